```python
import jax, jax.numpy as jnp
from jax import lax
import numpy as np

D_MODEL = 2048
BATCH = 2
SEQ = 4096
DEPTH = 1
DEC_BATCH = 8
DEC_SEQ = 4
PAST_LEN = 16384
PAGE_SIZE = 128

N_HEADS = 16
HEAD_DIM = 64
D_ATTN = N_HEADS * HEAD_DIM
D_CONV = D_MODEL // 2
CONV_W = 3
N_EXPERTS = 32
TOP_K = 4
D_FF = D_MODEL
SWIGLU_LIMIT = 7.0
SWIGLU_ALPHA = 1.702
SB_BIAS_INIT = -6.0
Q_BLOCK = 128
MOE_BLOCK = 128
RMS_EPS = 1e-5
SPLIT_SIZES = (D_CONV, D_CONV, D_CONV, D_ATTN, D_ATTN, D_ATTN, D_MODEL, D_MODEL)
D_IN = 3 * D_CONV + 3 * D_ATTN + 2 * D_MODEL

kernel_name = "hybrid_shortconv_stickbreaking_moe_step"


def rmsnorm(x, g):
    xf = x.astype(jnp.float32)
    inv = lax.rsqrt(jnp.mean(xf * xf, axis=-1, keepdims=True) + RMS_EPS)
    return (xf * inv * g.astype(jnp.float32)).astype(x.dtype)


def split_proj(z):
    offs = [int(o) for o in np.cumsum(SPLIT_SIZES)[:-1]]
    return jnp.split(z, offs, axis=-1)


def short_conv(u_full, conv_w, t_new):
    y = conv_w[0] * u_full[:, 0:t_new]
    for j in range(1, CONV_W):
        y = y + conv_w[j] * u_full[:, j:j + t_new]
    return y


def stick_breaking(q, k, v, sb_bias, q_pos, k_pos):
    z = (jnp.einsum("bqhd,bkhd->bhqk", q, k).astype(jnp.float32) * (HEAD_DIM ** -0.5)
         + sb_bias.astype(jnp.float32)[None, :, None, None])
    causal = k_pos[None, :] < q_pos[:, None]
    log_rem = jnp.where(causal, jax.nn.log_sigmoid(-z), 0.0)
    suffix = lax.cumsum(log_rem, axis=3, reverse=True) - log_rem
    log_a = jax.nn.log_sigmoid(z) + suffix
    a = jnp.where(causal, jnp.exp(log_a), 0.0)
    return jnp.einsum("bhqk,bkhd->bqhd", a.astype(v.dtype), v)


def sb_prompt(q, k, v, sb_bias):
    b, s = q.shape[0], q.shape[1]
    n_blk = s // Q_BLOCK
    qb = q.reshape(b, n_blk, Q_BLOCK, N_HEADS, HEAD_DIM).transpose(1, 0, 2, 3, 4)
    k_pos = jnp.arange(s, dtype=jnp.int32)

    def one_block(args):
        q_blk, i = args
        q_pos = i * Q_BLOCK + jnp.arange(Q_BLOCK, dtype=jnp.int32)
        return stick_breaking(q_blk, k, v, sb_bias, q_pos, k_pos)

    out = lax.map(one_block, (qb, jnp.arange(n_blk, dtype=jnp.int32)))
    return out.transpose(1, 0, 2, 3, 4).reshape(b, s, N_HEADS, HEAD_DIM)


def moe_ffn(x, router_w, router_b, w1, b1, w2, b2):
    t, d = x.shape
    logits = (x @ router_w).astype(jnp.float32) + router_b.astype(jnp.float32)
    top_v, top_i = lax.top_k(logits, TOP_K)
    gates = jax.nn.softmax(top_v, axis=-1)
    n_assign = t * TOP_K
    flat_e = top_i.reshape(-1).astype(jnp.int32)
    flat_tok = jnp.repeat(jnp.arange(t, dtype=jnp.int32), TOP_K)
    flat_g = gates.reshape(-1)
    order = jnp.argsort(flat_e)
    sorted_e = flat_e[order]
    counts = jnp.zeros((N_EXPERTS,), jnp.int32).at[flat_e].add(1)
    offsets = jnp.cumsum(counts) - counts
    padded = (counts + MOE_BLOCK - 1) // MOE_BLOCK * MOE_BLOCK
    padded_end = jnp.cumsum(padded)
    padded_off = padded_end - padded
    dest = padded_off[sorted_e] + (jnp.arange(n_assign, dtype=jnp.int32) - offsets[sorted_e])
    n_blocks = -(-n_assign // MOE_BLOCK) + N_EXPERTS
    n_rows = n_blocks * MOE_BLOCK
    row_tok = jnp.full((n_rows,), t, jnp.int32).at[dest].set(flat_tok[order])
    row_gate = jnp.zeros((n_rows,), jnp.float32).at[dest].set(flat_g[order])
    block_start = jnp.arange(n_blocks, dtype=jnp.int32) * MOE_BLOCK
    block_expert = jnp.minimum(jnp.searchsorted(padded_end, block_start, side="right"), N_EXPERTS - 1).astype(jnp.int32)
    x_pad = jnp.concatenate([x, jnp.zeros((1, d), x.dtype)], axis=0)

    def expert_block(args):
        tok, e = args
        xb = x_pad[tok]
        h = xb @ w1[e] + b1[e]
        h_glu = jnp.minimum(h[:, :D_FF], SWIGLU_LIMIT)
        h_lin = jnp.clip(h[:, D_FF:], -SWIGLU_LIMIT, SWIGLU_LIMIT)
        act = h_glu * jax.nn.sigmoid(SWIGLU_ALPHA * h_glu) * (h_lin + 1.0)
        return act @ w2[e] + b2[e]

    out = lax.map(expert_block, (row_tok.reshape(n_blocks, MOE_BLOCK), block_expert))
    out = out.reshape(n_rows, d) * row_gate[:, None].astype(x.dtype)
    return jnp.zeros((t + 1, d), x.dtype).at[row_tok].add(out)[:t]


def decoder_layer(x, conv_hist, attend, norm_mix, w_in, conv_w, sb_bias, w_br_conv, w_br_attn, w_o,
                  norm_ffn, router_w, router_b, w_mlp1, b_mlp1, w_mlp2, b_mlp2):
    b, t, d = x.shape
    xn = rmsnorm(x, norm_mix)
    cb, cc, ch, q, k, v, ga, gb = split_proj(xn @ w_in)
    u = cc * ch
    u_full = jnp.concatenate([conv_hist, u], axis=1)
    y_conv = cb * short_conv(u_full, conv_w, t)
    new_hist = u_full[:, -(CONV_W - 1):]
    q = q.reshape(b, t, N_HEADS, HEAD_DIM)
    k = k.reshape(b, t, N_HEADS, HEAD_DIM)
    v = v.reshape(b, t, N_HEADS, HEAD_DIM)
    y_attn = attend(q, k, v, sb_bias).reshape(b, t, D_ATTN)
    merged = jax.nn.sigmoid(ga) * (y_conv @ w_br_conv) + jax.nn.sigmoid(gb) * (y_attn @ w_br_attn)
    h = x + merged @ w_o
    hn = rmsnorm(h, norm_ffn).reshape(b * t, d)
    y = h + moe_ffn(hn, router_w, router_b, w_mlp1, b_mlp1, w_mlp2, b_mlp2).reshape(b, t, d)
    return y, k, v, new_hist


def setup_inputs(seed: int = 0) -> dict:
    key = jax.random.key(seed)
    ks = jax.random.split(key, 24)
    n_pages = PAST_LEN // PAGE_SIZE
    n_used = DEC_BATCH * n_pages
    n_pool = n_used + max(1, n_used // 4)
    nrm = lambda k, shape, scale: jax.random.normal(k, shape, jnp.float32) * scale
    page_table = jax.random.permutation(ks[0], n_pool)[:n_used].reshape(DEC_BATCH, n_pages).astype(jnp.int32)
    return {
        "x_prompt": nrm(ks[1], (BATCH, SEQ, D_MODEL), 1.0),
        "x_sample": nrm(ks[2], (DEC_BATCH, DEC_SEQ, D_MODEL), 1.0),
        "cache_k": nrm(ks[3], (DEPTH, n_pool, PAGE_SIZE, N_HEADS, HEAD_DIM), 1.0),
        "cache_v": nrm(ks[4], (DEPTH, n_pool, PAGE_SIZE, N_HEADS, HEAD_DIM), 1.0),
        "state_conv": nrm(ks[5], (DEPTH, DEC_BATCH, CONV_W - 1, D_CONV), 1.0),
        "page_table": page_table,
        "norm_mix": 1.0 + nrm(ks[6], (DEPTH, D_MODEL), 0.02),
        "w_in": nrm(ks[7], (DEPTH, D_MODEL, D_IN), D_MODEL ** -0.5),
        "conv_w": nrm(ks[8], (DEPTH, CONV_W, D_CONV), CONV_W ** -0.5),
        "sb_bias": SB_BIAS_INIT + nrm(ks[20], (DEPTH, N_HEADS), 0.1),
        "w_br_conv": nrm(ks[9], (DEPTH, D_CONV, D_MODEL), D_CONV ** -0.5),
        "w_br_attn": nrm(ks[10], (DEPTH, D_ATTN, D_MODEL), D_ATTN ** -0.5),
        "w_o": nrm(ks[11], (DEPTH, D_MODEL, D_MODEL), D_MODEL ** -0.5),
        "norm_ffn": 1.0 + nrm(ks[12], (DEPTH, D_MODEL), 0.02),
        "router_w": nrm(ks[13], (DEPTH, D_MODEL, N_EXPERTS), D_MODEL ** -0.5),
        "router_b": nrm(ks[14], (DEPTH, N_EXPERTS), 0.01),
        "w_mlp1": nrm(ks[15], (DEPTH, N_EXPERTS, D_MODEL, 2 * D_FF), D_MODEL ** -0.5),
        "b_mlp1": nrm(ks[16], (DEPTH, N_EXPERTS, 2 * D_FF), 0.01),
        "w_mlp2": nrm(ks[17], (DEPTH, N_EXPERTS, D_FF, D_MODEL), D_FF ** -0.5),
        "b_mlp2": nrm(ks[18], (DEPTH, N_EXPERTS, D_MODEL), 0.01),
        "norm_final": 1.0 + nrm(ks[19], (D_MODEL,), 0.02),
    }


def reference(x_prompt, x_sample, cache_k, cache_v, state_conv, page_table, norm_mix, w_in, conv_w,
              sb_bias, w_br_conv, w_br_attn, w_o, norm_ffn, router_w, router_b, w_mlp1, b_mlp1, w_mlp2,
              b_mlp2, norm_final):
    dec_b, dec_t = x_sample.shape[0], x_sample.shape[1]
    n_pages = page_table.shape[1]
    past_len = n_pages * cache_k.shape[2]
    yp, ys = x_prompt, x_sample
    kp, vp, cp, ksm, vsm, csm = [], [], [], [], [], []
    for l in range(DEPTH):
        params = (norm_mix[l], w_in[l], conv_w[l], sb_bias[l], w_br_conv[l], w_br_attn[l], w_o[l], norm_ffn[l],
                  router_w[l], router_b[l], w_mlp1[l], b_mlp1[l], w_mlp2[l], b_mlp2[l])
        zero_hist = jnp.zeros((yp.shape[0], CONV_W - 1, D_CONV), yp.dtype)
        yp, k1, v1, c1 = decoder_layer(yp, zero_hist, sb_prompt, *params)

        def attend_sample(q, k, v, bias, l=l):
            k_past = cache_k[l][page_table].reshape(dec_b, past_len, N_HEADS, HEAD_DIM)
            v_past = cache_v[l][page_table].reshape(dec_b, past_len, N_HEADS, HEAD_DIM)
            k_all = jnp.concatenate([k_past, k.astype(k_past.dtype)], axis=1)
            v_all = jnp.concatenate([v_past, v.astype(v_past.dtype)], axis=1)
            q_pos = past_len + jnp.arange(dec_t, dtype=jnp.int32)
            k_pos = jnp.arange(past_len + dec_t, dtype=jnp.int32)
            return stick_breaking(q, k_all, v_all, bias, q_pos, k_pos).astype(q.dtype)

        ys, k2, v2, c2 = decoder_layer(ys, state_conv[l].astype(ys.dtype), attend_sample, *params)
        kp.append(k1); vp.append(v1); cp.append(c1)
        ksm.append(k2); vsm.append(v2); csm.append(c2)
    y_prompt = rmsnorm(yp, norm_final)
    y_sample = rmsnorm(ys, norm_final)
    return (y_prompt, y_sample, jnp.stack(kp), jnp.stack(vp), jnp.stack(cp), jnp.stack(ksm), jnp.stack(vsm), jnp.stack(csm))
```

```python
import functools

import jax
import jax.numpy as jnp
from jax import lax
from jax.experimental import pallas as pl
from jax.experimental.pallas import tpu as pltpu

F32 = jnp.float32
BF16 = jnp.bfloat16

D_MODEL = 2048
N_HEADS = 16
HEAD_DIM = 64
D_ATTN = N_HEADS * HEAD_DIM
D_CONV = D_MODEL // 2
CONV_W = 3
N_EXPERTS = 32
TOP_K = 4
D_FF = D_MODEL
SWIGLU_LIMIT = 7.0
SWIGLU_ALPHA = 1.702
RMS_EPS = 1e-5
D_IN = 3 * D_CONV + 3 * D_ATTN + 2 * D_MODEL

LANES = 128
SUBLANES = 8
VMEM_LIMIT = 56 * 1024 * 1024
NEG_BIG = -3.0e38
HIGHEST = lax.Precision.HIGHEST


def _cparams(sem):
    return pltpu.CompilerParams(dimension_semantics=sem, vmem_limit_bytes=VMEM_LIMIT)


def _dot(a, b, hi=False):
    return jnp.dot(a, b, preferred_element_type=F32, precision=HIGHEST if hi else None)


IN_TN = 512
_IN_GROUPS = ((0, 3 * D_CONV // IN_TN),
              (3 * D_CONV // IN_TN, D_ATTN // IN_TN),
              ((3 * D_CONV + D_ATTN) // IN_TN, D_ATTN // IN_TN),
              ((3 * D_CONV + 2 * D_ATTN) // IN_TN, D_ATTN // IN_TN),
              ((3 * D_CONV + 3 * D_ATTN) // IN_TN, 2 * D_MODEL // IN_TN))


def _inproj_kernel(x_ref, g_ref, w_ref, *refs, hi):
    out_refs, xn_ref = refs[:-1], refs[-1]
    j = pl.program_id(1)

    @pl.when(j == 0)
    def _():
        x = x_ref[...]
        inv = lax.rsqrt(jnp.mean(x * x, axis=-1, keepdims=True) + RMS_EPS)
        xn_ref[...] = (x * inv * g_ref[...]).astype(xn_ref.dtype)

    acc = _dot(xn_ref[...], w_ref[...], hi)
    for o_ref, (a, n) in zip(out_refs, _IN_GROUPS):
        @pl.when((j >= a) & (j < a + n))
        def _(o_ref=o_ref):
            o_ref[...] = acc


def _inproj(x, g, w, *, tm, hi):
    t = x.shape[0]
    n_j = D_IN // IN_TN

    def omap(a, n):
        return lambda i, j: (i, jnp.clip(j - a, 0, n - 1))

    return pl.pallas_call(
        functools.partial(_inproj_kernel, hi=hi),
        grid=(t // tm, n_j),
        in_specs=[pl.BlockSpec((tm, D_MODEL), lambda i, j: (i, 0)),
                  pl.BlockSpec((1, D_MODEL), lambda i, j: (0, 0)),
                  pl.BlockSpec((D_MODEL, IN_TN), lambda i, j: (0, j))],
        out_specs=[pl.BlockSpec((tm, IN_TN), omap(a, n)) for a, n in _IN_GROUPS],
        out_shape=[jax.ShapeDtypeStruct((t, n * IN_TN), F32) for _, n in _IN_GROUPS],
        scratch_shapes=[pltpu.VMEM((tm, D_MODEL), F32 if hi else BF16)],
        compiler_params=_cparams(("parallel", "arbitrary")),
        name="inproj_hi" if hi else "inproj",
    )(x, g, w)


def _suffix_matrix(tk):
    m = lax.broadcasted_iota(jnp.int32, (2 * tk, tk), 0) % tk
    j = lax.broadcasted_iota(jnp.int32, (2 * tk, tk), 1)
    return (m > j).astype(BF16)


def _sb_block(z, carry, u2, mask):
    sp = jnp.maximum(z, 0.0) + jnp.log(1.0 + jnp.exp(-jnp.abs(z)))
    log_beta = z - sp
    if mask is not None:
        sp = jnp.where(mask, sp, 0.0)
    hi = sp.astype(BF16)
    lo = (sp - hi.astype(F32)).astype(BF16)
    s = jnp.dot(jnp.concatenate([hi, lo], axis=1), u2, preferred_element_type=F32)
    a = jnp.exp(log_beta - s - carry)
    if mask is not None:
        a = jnp.where(mask, a, 0.0)
    return a, carry + (s + sp)[:, 0:1]


ATT_T = 256


def _attn_prompt_kernel(q_ref, k_ref, v_ref, bias_ref, u2_ref, o_ref, kb_ref, vb_ref, acc_ref, carry_ref):
    qi = pl.program_id(2)
    t = ATT_T

    @pl.when(qi == 0)
    def _():
        kb_ref[...] = k_ref[...].astype(BF16)
        vb_ref[...] = v_ref[...].astype(BF16)

    q = q_ref[...] * (HEAD_DIM ** -0.5)
    lane = lax.broadcasted_iota(jnp.int32, (t, LANES), 1)
    q2 = jnp.concatenate([jnp.where(lane < HEAD_DIM, q, 0.0), jnp.where(lane >= HEAD_DIM, q, 0.0)],
                         axis=0).astype(BF16)
    bias = bias_ref[0]
    u2 = u2_ref[...]

    def block(j, mask):
        off = pl.multiple_of(j * t, t)
        kb = kb_ref[pl.ds(off, t), :]
        vb = vb_ref[pl.ds(off, t), :]
        z = lax.dot_general(q2, kb, (((1,), (1,)), ((), ())), preferred_element_type=F32) + bias
        a, carry = _sb_block(z, carry_ref[...], u2, mask)
        carry_ref[...] = carry
        acc_ref[...] += jnp.dot(a.astype(BF16), vb, preferred_element_type=F32)

    carry_ref[...] = jnp.zeros_like(carry_ref)
    acc_ref[...] = jnp.zeros_like(acc_ref)
    row = lax.broadcasted_iota(jnp.int32, (2 * t, t), 0) % t
    col = lax.broadcasted_iota(jnp.int32, (2 * t, t), 1)
    block(qi, col < row)

    def body(jj, c):
        block(qi - 1 - jj, None)
        return c

    lax.fori_loop(0, qi, body, 0)
    acc = acc_ref[...]
    o_ref[...] = jnp.where(lane < HEAD_DIM, acc[:t], acc[t:])


def _attn_prompt(q, k, v, sb_bias, batch, seq):
    t = ATT_T
    n_q = seq // t
    n_hp = N_HEADS // 2
    bias = jnp.broadcast_to(jnp.repeat(sb_bias.astype(F32).reshape(n_hp, 2), t, axis=1)[:, :, None],
                            (n_hp, 2 * t, t))
    return pl.pallas_call(
        _attn_prompt_kernel,
        grid=(batch, n_hp, n_q),
        in_specs=[pl.BlockSpec((t, LANES), lambda b, h, i: (b * n_q + i, h)),
                  pl.BlockSpec((seq, LANES), lambda b, h, i: (b, h)),
                  pl.BlockSpec((seq, LANES), lambda b, h, i: (b, h)),
                  pl.BlockSpec((1, 2 * t, t), lambda b, h, i: (h, 0, 0)),
                  pl.BlockSpec((2 * t, t), lambda b, h, i: (0, 0))],
        out_specs=pl.BlockSpec((t, LANES), lambda b, h, i: (b * n_q + i, h)),
        out_shape=jax.ShapeDtypeStruct((batch * seq, D_ATTN), F32),
        scratch_shapes=[pltpu.VMEM((seq, LANES), BF16), pltpu.VMEM((seq, LANES), BF16),
                        pltpu.VMEM((2 * t, LANES), F32), pltpu.VMEM((2 * t, 1), F32)],
        compiler_params=_cparams(("parallel", "parallel", "arbitrary")),
        name="attn_prompt",
    )(q, k, v, bias, _suffix_matrix(t))


def _attn_sample_kernel(pt_ref, q_ref, kn_ref, vn_ref, kc_ref, vc_ref, bias_ref, u2_ref, o_ref,
                        qm_ref, acc_ref, carry_ref, *, dec_t):
    p = pl.program_id(1)
    rows = N_HEADS * dec_t
    head_of_row = lax.broadcasted_iota(jnp.int32, (rows, D_ATTN), 0) // dec_t
    head_of_lane = lax.broadcasted_iota(jnp.int32, (rows, D_ATTN), 1) // HEAD_DIM
    own_head = head_of_row == head_of_lane
    u2 = u2_ref[...]

    def block(k_blk, v_blk, mask):
        z = lax.dot_general(qm_ref[...], k_blk, (((1,), (1,)), ((), ())),
                            preferred_element_type=F32, precision=HIGHEST) + bias_ref[...]
        a, carry = _sb_block(z, carry_ref[...], u2, mask)
        carry_ref[...] = carry
        acc_ref[...] += _dot(a, v_blk, hi=True)

    @pl.when(p == 0)
    def _():
        qm_ref[...] = jnp.where(own_head, q_ref[0] * (HEAD_DIM ** -0.5), 0.0)
        carry_ref[...] = jnp.zeros_like(carry_ref)
        acc_ref[...] = jnp.zeros_like(acc_ref)
        qpos = lax.broadcasted_iota(jnp.int32, (rows, LANES), 0) % dec_t
        kpos = lax.broadcasted_iota(jnp.int32, (rows, LANES), 1)
        block(kn_ref[0], vn_ref[0], kpos < qpos)

    block(kc_ref[0], vc_ref[0], None)

    @pl.when(p == pl.num_programs(1) - 1)
    def _():
        sel = (lax.broadcasted_iota(jnp.int32, (SUBLANES, rows), 1) % dec_t
               == lax.broadcasted_iota(jnp.int32, (SUBLANES, rows), 0)).astype(F32)
        o_ref[0] = _dot(sel, jnp.where(own_head, acc_ref[...], 0.0), hi=True)


def _attn_sample(q, k, v, cache_k, cache_v, page_table, sb_bias):
    dec_b, dec_t, _ = q.shape
    n_pages = page_table.shape[1]
    page = cache_k.shape[1]
    assert page == LANES and dec_t <= SUBLANES
    rows = N_HEADS * dec_t
    q_rep = jnp.tile(q, (1, N_HEADS, 1))
    pad = ((0, 0), (0, page - dec_t), (0, 0))
    k_new = jnp.pad(k, pad)
    v_new = jnp.pad(v, pad)
    bias = jnp.broadcast_to(jnp.repeat(sb_bias.astype(F32), dec_t)[:, None], (rows, page))

    def page_map(b, p, pt):
        return (pt[b * n_pages + (n_pages - 1 - p)], 0, 0)

    out = pl.pallas_call(
        functools.partial(_attn_sample_kernel, dec_t=dec_t),
        grid_spec=pltpu.PrefetchScalarGridSpec(
            num_scalar_prefetch=1,
            grid=(dec_b, n_pages),
            in_specs=[pl.BlockSpec((1, rows, D_ATTN), lambda b, p, pt: (b, 0, 0)),
                      pl.BlockSpec((1, page, D_ATTN), lambda b, p, pt: (b, 0, 0)),
                      pl.BlockSpec((1, page, D_ATTN), lambda b, p, pt: (b, 0, 0)),
                      pl.BlockSpec((1, page, D_ATTN), page_map),
                      pl.BlockSpec((1, page, D_ATTN), page_map),
                      pl.BlockSpec((rows, page), lambda b, p, pt: (0, 0)),
                      pl.BlockSpec((2 * page, page), lambda b, p, pt: (0, 0))],
            out_specs=pl.BlockSpec((1, SUBLANES, D_ATTN), lambda b, p, pt: (b, 0, 0)),
            scratch_shapes=[pltpu.VMEM((rows, D_ATTN), F32), pltpu.VMEM((rows, D_ATTN), F32),
                            pltpu.VMEM((rows, 1), F32)]),
        out_shape=jax.ShapeDtypeStruct((dec_b, SUBLANES, D_ATTN), F32),
        compiler_params=_cparams(("parallel", "arbitrary")),
        name="attn_sample",
    )(page_table.reshape(-1), q_rep, k_new, v_new, cache_k, cache_v, bias, _suffix_matrix(page))
    return out[:, :dec_t]


def _top4(logits):
    lane = lax.broadcasted_iota(jnp.int32, logits.shape, 1)
    vals, idxs = [], []
    l = logits
    for _ in range(TOP_K):
        m = jnp.max(l, axis=-1, keepdims=True)
        idx = jnp.min(jnp.where(l == m, lane, LANES), axis=-1, keepdims=True)
        vals.append(m)
        idxs.append(idx)
        l = jnp.where(lane == idx, NEG_BIG, l)
    es = [jnp.exp(v - vals[0]) for v in vals]
    inv = 1.0 / (es[0] + es[1] + es[2] + es[3])
    out_i = jnp.zeros(logits.shape, jnp.int32)
    out_g = jnp.zeros(logits.shape, F32)
    for k in range(TOP_K):
        out_i = jnp.where(lane == k, idxs[k], out_i)
        out_g = jnp.where(lane == k, es[k] * inv, out_g)
    return out_i, out_g


def _merge_kernel(*refs, hi, tm, seq_len, tail):
    if seq_len >= tm:
        cb_ref, cc_ref, ch_ref, pcc_ref, pch_ref = refs[:5]
    else:
        cb_ref, cc_ref, ch_ref, h1_ref, h2_ref = refs[:5]
    (ya_ref, ga_ref, gb_ref, x_ref, cw_ref, wc_ref, wa_ref, wo_ref, gn_ref, wr_ref, rb_ref,
     h_ref, hn_ref, ti_ref, tg_ref, ut_ref) = refs[5:]
    i = pl.program_id(0)
    mm = F32 if hi else BF16

    u = cc_ref[...] * ch_ref[...]
    row = lax.broadcasted_iota(jnp.int32, u.shape, 0)
    if seq_len >= tm:
        at_start = (i % (seq_len // tm)) == 0
        up = jnp.where(at_start, 0.0, pcc_ref[...] * pch_ref[...])
        u1 = jnp.where(row == 0, up[SUBLANES - 1:SUBLANES], pltpu.roll(u, 1, axis=0))
        u2 = jnp.where(row == 0, up[SUBLANES - 2:SUBLANES - 1],
                       jnp.where(row == 1, up[SUBLANES - 1:SUBLANES], pltpu.roll(u, 2, axis=0)))
    else:
        u1 = jnp.where(row % seq_len == 0, h1_ref[...], pltpu.roll(u, 1, axis=0))
        u2 = jnp.where(row % seq_len < 2, h2_ref[...], pltpu.roll(u, 2, axis=0))
    ut_ref[...] = u[tm - tail:]
    cw = cw_ref[...]
    y_conv = cb_ref[...] * (cw[0:1] * u2 + cw[1:2] * u1 + cw[2:3] * u)

    merged = (jax.nn.sigmoid(ga_ref[...]) * _dot(y_conv.astype(mm), wc_ref[...], hi)
              + jax.nn.sigmoid(gb_ref[...]) * _dot(ya_ref[...].astype(mm), wa_ref[...], hi))
    h = x_ref[...] + _dot(merged.astype(mm), wo_ref[...], hi)
    h_ref[...] = h
    inv = lax.rsqrt(jnp.mean(h * h, axis=-1, keepdims=True) + RMS_EPS)
    hn = h * inv * gn_ref[...]
    hn_ref[...] = hn.astype(BF16)
    logits = _dot(hn.astype(mm), wr_ref[...], hi) + rb_ref[...]
    ti, tg = _top4(logits)
    ti_ref[...] = ti
    tg_ref[...] = tg


def _merge(conv3, prev_or_hist, ya, gates, x, conv_w, wc, wa, wo, gn, wr, rb, *, tm, hi, seq_len):
    t = x.shape[0]
    n_i = t // tm
    tail = min(tm, SUBLANES) if seq_len >= tm else tm
    const = lambda i: (0, 0)
    once = pl.Buffered(1)
    if seq_len >= tm:
        r = tm // SUBLANES
        prev_map_c = lambda i: (jnp.maximum(i * r - 1, 0), 1)
        prev_map_h = lambda i: (jnp.maximum(i * r - 1, 0), 2)
        hist_specs = [pl.BlockSpec((SUBLANES, D_CONV), prev_map_c), pl.BlockSpec((SUBLANES, D_CONV), prev_map_h)]
        hist_args = (conv3, conv3)
    else:
        hist_specs = [pl.BlockSpec((tm, D_CONV), lambda i: (i, 0))] * 2
        hist_args = prev_or_hist
    in_specs = ([pl.BlockSpec((tm, D_CONV), lambda i: (i, 0)),
                 pl.BlockSpec((tm, D_CONV), lambda i: (i, 1)),
                 pl.BlockSpec((tm, D_CONV), lambda i: (i, 2))]
                + hist_specs
                + [pl.BlockSpec((tm, D_ATTN), lambda i: (i, 0)),
                   pl.BlockSpec((tm, D_MODEL), lambda i: (i, 0)),
                   pl.BlockSpec((tm, D_MODEL), lambda i: (i, 1)),
                   pl.BlockSpec((tm, D_MODEL), lambda i: (i, 0)),
                   pl.BlockSpec((SUBLANES, D_CONV), const),
                   pl.BlockSpec((D_CONV, D_MODEL), const, pipeline_mode=once),
                   pl.BlockSpec((D_ATTN, D_MODEL), const, pipeline_mode=once),
                   pl.BlockSpec((D_MODEL, D_MODEL), const, pipeline_mode=once),
                   pl.BlockSpec((1, D_MODEL), const),
                   pl.BlockSpec((D_MODEL, LANES), const),
                   pl.BlockSpec((1, LANES), const)])
    return pl.pallas_call(
        functools.partial(_merge_kernel, hi=hi, tm=tm, seq_len=seq_len, tail=tail),
        grid=(n_i,),
        in_specs=in_specs,
        out_specs=[pl.BlockSpec((tm, D_MODEL), lambda i: (i, 0)),
                   pl.BlockSpec((tm, D_MODEL), lambda i: (i, 0)),
                   pl.BlockSpec((tm, LANES), lambda i: (i, 0)),
                   pl.BlockSpec((tm, LANES), lambda i: (i, 0)),
                   pl.BlockSpec((tail, D_CONV), lambda i: (i, 0))],
        out_shape=[jax.ShapeDtypeStruct((t, D_MODEL), F32),
                   jax.ShapeDtypeStruct((t, D_MODEL), BF16),
                   jax.ShapeDtypeStruct((t, LANES), jnp.int32),
                   jax.ShapeDtypeStruct((t, LANES), F32),
                   jax.ShapeDtypeStruct((n_i * tail, D_CONV), F32)],
        compiler_params=_cparams(("parallel",)),
        name="merge_hi" if hi else "merge",
    )(conv3, conv3, conv3, *hist_args, ya, gates, gates, x, conv_w, wc, wa, wo, gn, wr, rb)


MOE_TM = 256
MOE_TF = 512
MOE_TN = 512


def _expert_changed(be_ref, i):
    return (i == 0) | (be_ref[i] != be_ref[jnp.maximum(i - 1, 0)])


def _moe_up_kernel(be_ref, nu_ref, x_ref, wg_ref, wl_ref, bg_ref, bl_ref, o_ref, wgb_ref, wlb_ref):
    i = pl.program_id(1)

    @pl.when(_expert_changed(be_ref, i))
    def _():
        wgb_ref[...] = wg_ref[0].astype(BF16)
        wlb_ref[...] = wl_ref[0].astype(BF16)

    @pl.when(i < nu_ref[0])
    def _():
        x = x_ref[...]
        h_glu = jnp.minimum(_dot(x, wgb_ref[...]) + bg_ref[0], SWIGLU_LIMIT)
        h_lin = jnp.clip(_dot(x, wlb_ref[...]) + bl_ref[0], -SWIGLU_LIMIT, SWIGLU_LIMIT)
        o_ref[...] = (h_glu * jax.nn.sigmoid(SWIGLU_ALPHA * h_glu) * (h_lin + 1.0)).astype(BF16)

    @pl.when(i >= nu_ref[0])
    def _():
        o_ref[...] = jnp.zeros_like(o_ref)


def _moe_down_kernel(be_ref, nu_ref, a_ref, w_ref, b_ref, o_ref, wb_ref):
    i = pl.program_id(1)

    @pl.when(_expert_changed(be_ref, i))
    def _():
        wb_ref[...] = w_ref[0].astype(BF16)

    @pl.when(i < nu_ref[0])
    def _():
        o_ref[...] = _dot(a_ref[...], wb_ref[...]) + b_ref[0]

    @pl.when(i >= nu_ref[0])
    def _():
        o_ref[...] = jnp.zeros_like(o_ref)


def _moe_experts(xs, block_expert, n_used, w1, b1, w2, b2):
    n_rows = xs.shape[0]
    n_i = n_rows // MOE_TM
    n_f = D_FF // MOE_TF
    b1r = b1.reshape(N_EXPERTS, 1, 2 * D_FF)
    b2r = b2.reshape(N_EXPERTS, 1, D_MODEL)
    act = pl.pallas_call(
        _moe_up_kernel,
        grid_spec=pltpu.PrefetchScalarGridSpec(
            num_scalar_prefetch=2,
            grid=(n_f, n_i),
            in_specs=[pl.BlockSpec((MOE_TM, D_MODEL), lambda j, i, be, nu: (i, 0)),
                      pl.BlockSpec((1, D_MODEL, MOE_TF), lambda j, i, be, nu: (be[i], 0, j)),
                      pl.BlockSpec((1, D_MODEL, MOE_TF), lambda j, i, be, nu: (be[i], 0, n_f + j)),
                      pl.BlockSpec((1, 1, MOE_TF), lambda j, i, be, nu: (be[i], 0, j)),
                      pl.BlockSpec((1, 1, MOE_TF), lambda j, i, be, nu: (be[i], 0, n_f + j))],
            out_specs=pl.BlockSpec((MOE_TM, MOE_TF), lambda j, i, be, nu: (i, j)),
            scratch_shapes=[pltpu.VMEM((D_MODEL, MOE_TF), BF16), pltpu.VMEM((D_MODEL, MOE_TF), BF16)]),
        out_shape=jax.ShapeDtypeStruct((n_rows, D_FF), BF16),
        compiler_params=_cparams(("arbitrary", "arbitrary")),
        name="moe_up",
    )(block_expert, n_used, xs, w1, w1, b1r, b1r)
    return pl.pallas_call(
        _moe_down_kernel,
        grid_spec=pltpu.PrefetchScalarGridSpec(
            num_scalar_prefetch=2,
            grid=(D_MODEL // MOE_TN, n_i),
            in_specs=[pl.BlockSpec((MOE_TM, D_FF), lambda j, i, be, nu: (i, 0)),
                      pl.BlockSpec((1, D_FF, MOE_TN), lambda j, i, be, nu: (be[i], 0, j)),
                      pl.BlockSpec((1, 1, MOE_TN), lambda j, i, be, nu: (be[i], 0, j))],
            out_specs=pl.BlockSpec((MOE_TM, MOE_TN), lambda j, i, be, nu: (i, j)),
            scratch_shapes=[pltpu.VMEM((D_FF, MOE_TN), BF16)]),
        out_shape=jax.ShapeDtypeStruct((n_rows, D_MODEL), F32),
        compiler_params=_cparams(("arbitrary", "arbitrary")),
        name="moe_down",
    )(block_expert, n_used, act, w2, b2r)


def _route(top_i):
    t = top_i.shape[0]
    n_assign = t * TOP_K
    flat_e = top_i.reshape(-1)
    order = jnp.argsort(flat_e)
    sorted_e = flat_e[order]
    counts = jnp.zeros((N_EXPERTS,), jnp.int32).at[flat_e].add(1)
    offsets = jnp.cumsum(counts) - counts
    padded = (counts + MOE_TM - 1) // MOE_TM * MOE_TM
    padded_end = jnp.cumsum(padded)
    padded_off = padded_end - padded
    dest = padded_off[sorted_e] + (jnp.arange(n_assign, dtype=jnp.int32) - offsets[sorted_e])
    n_tiles = -(-n_assign // MOE_TM) + N_EXPERTS
    n_rows = n_tiles * MOE_TM
    row_tok = jnp.full((n_rows,), t, jnp.int32).at[dest].set((order // TOP_K).astype(jnp.int32))
    pos = jnp.zeros((n_assign,), jnp.int32).at[order].set(dest).reshape(t, TOP_K)
    tile_start = jnp.arange(n_tiles, dtype=jnp.int32) * MOE_TM
    block_expert = jnp.minimum(jnp.searchsorted(padded_end, tile_start, side="right"),
                               N_EXPERTS - 1).astype(jnp.int32)
    n_used = (padded_end[-1:] // MOE_TM).astype(jnp.int32)
    return row_tok, pos, block_expert, n_used


def _combine_kernel(h_ref, ys_ref, g_ref, gn_ref, o_ref):
    y = h_ref[...]
    g = g_ref[...]
    for k in range(TOP_K):
        y = y + g[:, k:k + 1] * ys_ref[k]
    inv = lax.rsqrt(jnp.mean(y * y, axis=-1, keepdims=True) + RMS_EPS)
    o_ref[...] = y * inv * gn_ref[...]


def _combine(h, ys4, g, gn, *, tm):
    t = h.shape[0]
    return pl.pallas_call(
        _combine_kernel,
        grid=(t // tm,),
        in_specs=[pl.BlockSpec((tm, D_MODEL), lambda i: (i, 0)),
                  pl.BlockSpec((TOP_K, tm, D_MODEL), lambda i: (0, i, 0)),
                  pl.BlockSpec((tm, LANES), lambda i: (i, 0)),
                  pl.BlockSpec((1, D_MODEL), lambda i: (0, 0))],
        out_specs=pl.BlockSpec((tm, D_MODEL), lambda i: (i, 0)),
        out_shape=jax.ShapeDtypeStruct((t, D_MODEL), F32),
        compiler_params=_cparams(("parallel",)),
        name="combine",
    )(h, ys4, g, gn)


def kernel(x_prompt, x_sample, cache_k, cache_v, state_conv, page_table, norm_mix, w_in, conv_w, sb_bias,
           w_br_conv, w_br_attn, w_o, norm_ffn, router_w, router_b, w_mlp1, b_mlp1, w_mlp2, b_mlp2, norm_final):
    assert w_in.shape[0] == 1, "single layer"
    batch, seq, d = x_prompt.shape
    dec_b, dec_t, _ = x_sample.shape
    n_pool, page = cache_k.shape[1], cache_k.shape[2]
    t_p, t_s = batch * seq, dec_b * dec_t
    assert dec_t > CONV_W - 1

    g_mix = norm_mix[0].reshape(1, d)
    g_ffn = norm_ffn[0].reshape(1, d)
    g_fin = norm_final.reshape(1, d)
    cw = jnp.pad(conv_w[0], ((0, SUBLANES - CONV_W), (0, 0)))
    wr = jnp.pad(router_w[0], ((0, 0), (0, LANES - N_EXPERTS)))
    rb = jnp.pad(router_b[0].astype(F32), (0, LANES - N_EXPERTS), constant_values=NEG_BIG).reshape(1, LANES)

    xp = x_prompt.reshape(t_p, d)
    conv3, q, k, v, gates = _inproj(xp, g_mix, w_in[0].astype(BF16), tm=512, hi=False)
    ya = _attn_prompt(q, k, v, sb_bias[0], batch, seq)
    h_p, hn_p, ti_p, tg_p, ut_p = _merge(
        conv3, None, ya, gates, xp, cw, w_br_conv[0].astype(BF16), w_br_attn[0].astype(BF16),
        w_o[0].astype(BF16), g_ffn, wr.astype(BF16), rb, tm=256, hi=False, seq_len=seq)

    xs = x_sample.reshape(t_s, d)
    conv3_s, q_s, k_s, v_s, gates_s = _inproj(xs, g_mix, w_in[0], tm=t_s, hi=True)
    ya_s = _attn_sample(q_s.reshape(dec_b, dec_t, D_ATTN), k_s.reshape(dec_b, dec_t, D_ATTN),
                        v_s.reshape(dec_b, dec_t, D_ATTN), cache_k[0].reshape(n_pool, page, D_ATTN),
                        cache_v[0].reshape(n_pool, page, D_ATTN), page_table, sb_bias[0])
    hist = state_conv[0].astype(F32)
    zero = jnp.zeros((dec_b, dec_t - 2, D_CONV), F32)
    hist1 = jnp.concatenate([hist[:, 1:2], jnp.zeros((dec_b, dec_t - 1, D_CONV), F32)], axis=1)
    hist2 = jnp.concatenate([hist, zero], axis=1)
    h_s, hn_s, ti_s, tg_s, u_s = _merge(
        conv3_s, (hist1.reshape(t_s, D_CONV), hist2.reshape(t_s, D_CONV)), ya_s.reshape(t_s, D_ATTN),
        gates_s, xs, cw, w_br_conv[0], w_br_attn[0], w_o[0], g_ffn, wr, rb, tm=t_s, hi=True, seq_len=dec_t)

    t_all = t_p + t_s
    top_i = jnp.concatenate([ti_p[:, :TOP_K], ti_s[:, :TOP_K]], axis=0)
    row_tok, pos, block_expert, n_used = _route(top_i)
    hn_all = jnp.concatenate([hn_p, hn_s, jnp.zeros((1, d), BF16)], axis=0)
    ys = _moe_experts(hn_all[row_tok], block_expert, n_used, w_mlp1[0], b_mlp1[0], w_mlp2[0], b_mlp2[0])
    ys4_p = ys[pos[:t_p].T]
    ys4_s = ys[pos[t_p:].T]
    y_p = _combine(h_p, ys4_p, tg_p, g_fin, tm=256)
    y_s = _combine(h_s, ys4_s, tg_s, g_fin, tm=t_s)

    n_hist = CONV_W - 1
    conv_p = ut_p.reshape(batch, seq // 256, SUBLANES, D_CONV)[:, -1, SUBLANES - n_hist:]
    conv_s = u_s.reshape(dec_b, dec_t, D_CONV)[:, dec_t - n_hist:]
    return (y_p.reshape(batch, seq, d), y_s.reshape(dec_b, dec_t, d),
            k.reshape(1, batch, seq, N_HEADS, HEAD_DIM), v.reshape(1, batch, seq, N_HEADS, HEAD_DIM),
            conv_p[None], k_s.reshape(1, dec_b, dec_t, N_HEADS, HEAD_DIM),
            v_s.reshape(1, dec_b, dec_t, N_HEADS, HEAD_DIM), conv_s[None])
```

```python
import functools

import jax
import jax.numpy as jnp
from jax import lax
from jax.experimental import pallas as pl
from jax.experimental.pallas import tpu as pltpu

F32 = jnp.float32
BF16 = jnp.bfloat16

D_MODEL = 2048
N_HEADS = 16
HEAD_DIM = 64
D_ATTN = N_HEADS * HEAD_DIM
D_CONV = D_MODEL // 2
CONV_W = 3
N_EXPERTS = 32
TOP_K = 4
D_FF = D_MODEL
SWIGLU_LIMIT = 7.0
SWIGLU_ALPHA = 1.702
RMS_EPS = 1e-5
D_IN = 3 * D_CONV + 3 * D_ATTN + 2 * D_MODEL

LANES = 128
SUBLANES = 8
VMEM_LIMIT = 56 * 1024 * 1024
NEG_BIG = -3.0e38


def _cparams(sem):
    return pltpu.CompilerParams(dimension_semantics=sem, vmem_limit_bytes=VMEM_LIMIT)


def _dot(a, b):
    return jnp.dot(a.astype(BF16), b.astype(BF16), preferred_element_type=F32)


IN_TN = 512
_IN_GROUPS = ((0, 3 * D_CONV // IN_TN),
              (3 * D_CONV // IN_TN, D_ATTN // IN_TN),
              ((3 * D_CONV + D_ATTN) // IN_TN, D_ATTN // IN_TN),
              ((3 * D_CONV + 2 * D_ATTN) // IN_TN, D_ATTN // IN_TN),
              ((3 * D_CONV + 3 * D_ATTN) // IN_TN, 2 * D_MODEL // IN_TN))


def _inproj_kernel(x_ref, g_ref, w_ref, *refs):
    out_refs, xn_ref = refs[:-1], refs[-1]
    j = pl.program_id(1)

    @pl.when(j == 0)
    def _():
        x = x_ref[...]
        inv = lax.rsqrt(jnp.mean(x * x, axis=-1, keepdims=True) + RMS_EPS)
        xn_ref[...] = (x * inv * g_ref[...]).astype(xn_ref.dtype)

    acc = _dot(xn_ref[...], w_ref[...])
    for o_ref, (a, n) in zip(out_refs, _IN_GROUPS):
        @pl.when((j >= a) & (j < a + n))
        def _(o_ref=o_ref):
            o_ref[...] = acc


def _inproj(x, g, w, *, tm):
    t = x.shape[0]
    n_j = D_IN // IN_TN

    def omap(a, n):
        return lambda i, j: (i, jnp.clip(j - a, 0, n - 1))

    return pl.pallas_call(
        _inproj_kernel,
        grid=(t // tm, n_j),
        in_specs=[pl.BlockSpec((tm, D_MODEL), lambda i, j: (i, 0)),
                  pl.BlockSpec((1, D_MODEL), lambda i, j: (0, 0)),
                  pl.BlockSpec((D_MODEL, IN_TN), lambda i, j: (0, j))],
        out_specs=[pl.BlockSpec((tm, IN_TN), omap(a, n)) for a, n in _IN_GROUPS],
        out_shape=[jax.ShapeDtypeStruct((t, n * IN_TN), F32) for _, n in _IN_GROUPS],
        scratch_shapes=[pltpu.VMEM((tm, D_MODEL), BF16)],
        compiler_params=_cparams(("parallel", "arbitrary")),
        name="inproj",
    )(x, g, w)


LOG2E = 1.4426950408889634
SP_LINEAR_ABOVE = 64.0


def _suffix_matrix(tk):
    m = lax.broadcasted_iota(jnp.int32, (2 * tk, tk), 0) % tk
    j = lax.broadcasted_iota(jnp.int32, (2 * tk, tk), 1)
    return (m > j).astype(BF16)


def _sb_scores(z, u2, mask):
    sp = jnp.where(z > SP_LINEAR_ABOVE, z, jnp.log(1.0 + jnp.exp2(z)) * LOG2E)
    log_beta = z - sp
    if mask is not None:
        sp = jnp.where(mask, sp, 0.0)
    hi = pltpu.bitcast(pltpu.bitcast(sp, jnp.int32) & jnp.int32(-65536), F32)
    lo = sp - hi
    s = jnp.dot(jnp.concatenate([hi.astype(BF16), lo.astype(BF16)], axis=1), u2, preferred_element_type=F32)
    pre = log_beta - s
    if mask is not None:
        pre = jnp.where(mask, pre, NEG_BIG)
    return pre, (s + sp)[:, 0:1]


ATT_T = 256


def _attn_prompt_kernel(q_ref, k_ref, v_ref, bias_ref, u2_ref, o_ref, kb_ref, vb_ref, acc_ref):
    qi = pl.program_id(2)
    t = ATT_T

    @pl.when(qi == 0)
    def _():
        kb_ref[...] = k_ref[...].astype(BF16)
        vb_ref[...] = v_ref[...].astype(BF16)

    q = q_ref[...] * (HEAD_DIM ** -0.5 * LOG2E)
    lane = lax.broadcasted_iota(jnp.int32, (t, LANES), 1)
    q2 = jnp.concatenate([jnp.where(lane < HEAD_DIM, q, 0.0), jnp.where(lane >= HEAD_DIM, q, 0.0)],
                         axis=0).astype(BF16)
    bias = bias_ref[0]
    u2 = u2_ref[...]

    def scores(j, mask):
        kb = kb_ref[pl.ds(pl.multiple_of(j * t, t), t), :]
        z = lax.dot_general(q2, kb, (((1,), (1,)), ((), ())), preferred_element_type=F32) + bias
        return _sb_scores(z, u2, mask)

    def accumulate(j, pre, carry):
        vb = vb_ref[pl.ds(pl.multiple_of(j * t, t), t), :]
        acc_ref[...] += jnp.dot(jnp.exp2(pre - carry).astype(BF16), vb, preferred_element_type=F32)

    acc_ref[...] = jnp.zeros_like(acc_ref)
    row = lax.broadcasted_iota(jnp.int32, (2 * t, t), 0) % t
    col = lax.broadcasted_iota(jnp.int32, (2 * t, t), 1)
    pre, tot = scores(qi, col < row)

    def body(jj, c):
        pre, tot, carry = c
        j = qi - 1 - jj
        nxt = scores(j, None)
        accumulate(j + 1, pre, carry)
        return nxt + (carry + tot,)

    pre, tot, carry = lax.fori_loop(0, qi, body, (pre, tot, jnp.zeros((2 * t, 1), F32)))
    accumulate(0, pre, carry)
    acc = acc_ref[...]
    o_ref[...] = jnp.where(lane < HEAD_DIM, acc[:t], acc[t:])


def _attn_prompt(q, k, v, sb_bias, batch, seq):
    t = ATT_T
    n_q = seq // t
    n_hp = N_HEADS // 2
    bias = jnp.broadcast_to(jnp.repeat(sb_bias.astype(F32).reshape(n_hp, 2) * LOG2E, t, axis=1)[:, :, None],
                            (n_hp, 2 * t, t))
    return pl.pallas_call(
        _attn_prompt_kernel,
        grid=(batch, n_hp, n_q),
        in_specs=[pl.BlockSpec((t, LANES), lambda b, h, i: (b * n_q + i, h)),
                  pl.BlockSpec((seq, LANES), lambda b, h, i: (b, h)),
                  pl.BlockSpec((seq, LANES), lambda b, h, i: (b, h)),
                  pl.BlockSpec((1, 2 * t, t), lambda b, h, i: (h, 0, 0)),
                  pl.BlockSpec((2 * t, t), lambda b, h, i: (0, 0))],
        out_specs=pl.BlockSpec((t, LANES), lambda b, h, i: (b * n_q + i, h)),
        out_shape=jax.ShapeDtypeStruct((batch * seq, D_ATTN), F32),
        scratch_shapes=[pltpu.VMEM((seq, LANES), BF16), pltpu.VMEM((seq, LANES), BF16),
                        pltpu.VMEM((2 * t, LANES), F32)],
        compiler_params=_cparams(("parallel", "parallel", "arbitrary")),
        name="attn_prompt",
    )(q, k, v, bias, _suffix_matrix(t))


SAMPLE_QR = 16
SAMPLE_PAGES_PER_STEP = 8


def _attn_sample_kernel(pt_ref, q_ref, kn_ref, vn_ref, *refs, page, pages_per_step):
    kc_refs, vc_refs = refs[:pages_per_step], refs[pages_per_step:2 * pages_per_step]
    bias_ref, u2_ref, o_ref, qb_ref, acc_ref, carry_ref = refs[2 * pages_per_step:]
    p = pl.program_id(1)
    qr = SAMPLE_QR
    u2 = u2_ref[...]

    def head_slab(ref, h):
        return ref[h * HEAD_DIM:(h + 1) * HEAD_DIM, :].astype(BF16)

    def scores(k_ref, mask):
        z = jnp.concatenate(
            [jnp.dot(qb_ref[h * qr:(h + 1) * qr, :], head_slab(k_ref, h), preferred_element_type=F32)
             for h in range(N_HEADS)], axis=0) * (HEAD_DIM ** -0.5 * LOG2E) + bias_ref[...]
        return _sb_scores(z, u2, mask)

    def sweep(k_refs, v_refs, mask):
        blocks = [scores(k_ref, mask) for k_ref in k_refs]
        carry = carry_ref[...]
        acc = [acc_ref[h * qr:(h + 1) * qr, :] for h in range(N_HEADS)]
        for (pre, tot), v_ref in zip(blocks, v_refs):
            a = jnp.exp2(pre - carry).astype(BF16)
            carry = carry + tot
            for h in range(N_HEADS):
                acc[h] = acc[h] + lax.dot_general(a[h * qr:(h + 1) * qr], head_slab(v_ref, h),
                                                  (((1,), (1,)), ((), ())), preferred_element_type=F32)
        carry_ref[...] = carry
        for h in range(N_HEADS):
            acc_ref[h * qr:(h + 1) * qr, :] = acc[h]

    @pl.when(p == 0)
    def _():
        qb_ref[...] = q_ref[...].astype(BF16)
        carry_ref[...] = jnp.zeros_like(carry_ref)
        acc_ref[...] = jnp.zeros_like(acc_ref)
        qpos = lax.broadcasted_iota(jnp.int32, (N_HEADS * qr, page), 0) % qr
        kpos = lax.broadcasted_iota(jnp.int32, (N_HEADS * qr, page), 1)
        sweep([kn_ref], [vn_ref], kpos < qpos)

    sweep(kc_refs, vc_refs, None)

    @pl.when(p == pl.num_programs(1) - 1)
    def _():
        o_ref[...] = acc_ref[...]


def _attn_sample(q, k, v, cache_k, cache_v, page_table, sb_bias):
    dec_b, dec_t = q.shape[:2]
    n_pool, page = cache_k.shape[:2]
    n_pages = page_table.shape[1]
    pps = SAMPLE_PAGES_PER_STEP
    qr = SAMPLE_QR
    assert page == LANES and dec_t <= qr and n_pages % pps == 0
    rows = N_HEADS * qr
    q_rows = jnp.pad(q.transpose(0, 2, 1, 3), ((0, 0), (0, 0), (0, qr - dec_t), (0, 0))).reshape(dec_b, rows, HEAD_DIM)
    pad = ((0, 0), (0, 0), (0, 0), (0, page - dec_t))
    k_new = jnp.pad(k.transpose(0, 2, 3, 1), pad).reshape(dec_b, D_ATTN, page)
    v_new = jnp.pad(v.transpose(0, 2, 3, 1), pad).reshape(dec_b, D_ATTN, page)
    bias = jnp.broadcast_to(jnp.repeat(sb_bias.astype(F32) * LOG2E, qr)[:, None], (rows, page))
    pool_t = lambda c: c.transpose(0, 2, 3, 1).reshape(n_pool, D_ATTN, page)
    page_block = (None, D_ATTN, page)

    def page_spec(s):
        return pl.BlockSpec(page_block, lambda b, p, pt: (pt[b * n_pages + n_pages - 1 - (p * pps + s)], 0, 0))

    per_seq = lambda b, p, pt: (b, 0, 0)
    out = pl.pallas_call(
        functools.partial(_attn_sample_kernel, page=page, pages_per_step=pps),
        grid_spec=pltpu.PrefetchScalarGridSpec(
            num_scalar_prefetch=1,
            grid=(dec_b, n_pages // pps),
            in_specs=([pl.BlockSpec((None, rows, HEAD_DIM), per_seq),
                       pl.BlockSpec(page_block, per_seq),
                       pl.BlockSpec(page_block, per_seq)]
                      + [page_spec(s) for s in range(pps)] * 2
                      + [pl.BlockSpec((rows, page), lambda b, p, pt: (0, 0)),
                         pl.BlockSpec((2 * page, page), lambda b, p, pt: (0, 0))]),
            out_specs=pl.BlockSpec((None, rows, HEAD_DIM), per_seq),
            scratch_shapes=[pltpu.VMEM((rows, HEAD_DIM), BF16), pltpu.VMEM((rows, HEAD_DIM), F32),
                            pltpu.VMEM((rows, 1), F32)]),
        out_shape=jax.ShapeDtypeStruct((dec_b, rows, HEAD_DIM), F32),
        compiler_params=_cparams(("parallel", "arbitrary")),
        name="attn_sample",
    )(page_table.reshape(-1), q_rows, k_new, v_new,
      *([pool_t(cache_k)] * pps), *([pool_t(cache_v)] * pps), bias, _suffix_matrix(page))
    return out.reshape(dec_b, N_HEADS, qr, HEAD_DIM)[:, :, :dec_t].transpose(0, 2, 1, 3)


def _top4(logits):
    lane = lax.broadcasted_iota(jnp.int32, logits.shape, 1)
    vals, idxs = [], []
    l = logits
    for _ in range(TOP_K):
        m = jnp.max(l, axis=-1, keepdims=True)
        idx = jnp.min(jnp.where(l == m, lane, LANES), axis=-1, keepdims=True)
        vals.append(m)
        idxs.append(idx)
        l = jnp.where(lane == idx, NEG_BIG, l)
    es = [jnp.exp(v - vals[0]) for v in vals]
    inv = 1.0 / (es[0] + es[1] + es[2] + es[3])
    out_i = jnp.zeros(logits.shape, jnp.int32)
    out_g = jnp.zeros(logits.shape, F32)
    for k in range(TOP_K):
        out_i = jnp.where(lane == k, idxs[k], out_i)
        out_g = jnp.where(lane == k, es[k] * inv, out_g)
    return out_i, out_g


def _merge_kernel(*refs, tm, seq_len, tail):
    if seq_len >= tm:
        cb_ref, cc_ref, ch_ref, pcc_ref, pch_ref = refs[:5]
    else:
        cb_ref, cc_ref, ch_ref, h1_ref, h2_ref = refs[:5]
    (ya_ref, ga_ref, gb_ref, x_ref, cw_ref, wc_ref, wa_ref, wo_ref, gn_ref, wr_ref, rb_ref,
     h_ref, hn_ref, ti_ref, tg_ref, ut_ref) = refs[5:]
    i = pl.program_id(0)

    u = cc_ref[...] * ch_ref[...]
    row = lax.broadcasted_iota(jnp.int32, u.shape, 0)
    if seq_len >= tm:
        at_start = (i % (seq_len // tm)) == 0
        up = jnp.where(at_start, 0.0, pcc_ref[...] * pch_ref[...])
        u1 = jnp.where(row == 0, up[SUBLANES - 1:SUBLANES], pltpu.roll(u, 1, axis=0))
        u2 = jnp.where(row == 0, up[SUBLANES - 2:SUBLANES - 1],
                       jnp.where(row == 1, up[SUBLANES - 1:SUBLANES], pltpu.roll(u, 2, axis=0)))
    else:
        u1 = jnp.where(row % seq_len == 0, h1_ref[...], pltpu.roll(u, 1, axis=0))
        u2 = jnp.where(row % seq_len < 2, h2_ref[...], pltpu.roll(u, 2, axis=0))
    ut_ref[...] = u[tm - tail:]
    cw = cw_ref[...]
    y_conv = cb_ref[...] * (cw[0:1] * u2 + cw[1:2] * u1 + cw[2:3] * u)

    merged = (jax.nn.sigmoid(ga_ref[...]) * _dot(y_conv, wc_ref[...])
              + jax.nn.sigmoid(gb_ref[...]) * _dot(ya_ref[...], wa_ref[...]))
    h = x_ref[...] + _dot(merged, wo_ref[...])
    h_ref[...] = h
    inv = lax.rsqrt(jnp.mean(h * h, axis=-1, keepdims=True) + RMS_EPS)
    hn = h * inv * gn_ref[...]
    hn_ref[...] = hn.astype(BF16)
    logits = _dot(hn, wr_ref[...]) + rb_ref[...]
    ti, tg = _top4(logits)
    ti_ref[...] = ti
    tg_ref[...] = tg


def _merge(conv3, prev_or_hist, ya, gates, x, conv_w, wc, wa, wo, gn, wr, rb, *, tm, seq_len):
    t = x.shape[0]
    n_i = t // tm
    tail = min(tm, SUBLANES) if seq_len >= tm else tm
    const = lambda i: (0, 0)
    once = pl.Buffered(1)
    if seq_len >= tm:
        r = tm // SUBLANES
        prev_map_c = lambda i: (jnp.maximum(i * r - 1, 0), 1)
        prev_map_h = lambda i: (jnp.maximum(i * r - 1, 0), 2)
        hist_specs = [pl.BlockSpec((SUBLANES, D_CONV), prev_map_c), pl.BlockSpec((SUBLANES, D_CONV), prev_map_h)]
        hist_args = (conv3, conv3)
    else:
        hist_specs = [pl.BlockSpec((tm, D_CONV), lambda i: (i, 0))] * 2
        hist_args = prev_or_hist
    in_specs = ([pl.BlockSpec((tm, D_CONV), lambda i: (i, 0)),
                 pl.BlockSpec((tm, D_CONV), lambda i: (i, 1)),
                 pl.BlockSpec((tm, D_CONV), lambda i: (i, 2))]
                + hist_specs
                + [pl.BlockSpec((tm, D_ATTN), lambda i: (i, 0)),
                   pl.BlockSpec((tm, D_MODEL), lambda i: (i, 0)),
                   pl.BlockSpec((tm, D_MODEL), lambda i: (i, 1)),
                   pl.BlockSpec((tm, D_MODEL), lambda i: (i, 0)),
                   pl.BlockSpec((SUBLANES, D_CONV), const),
                   pl.BlockSpec((D_CONV, D_MODEL), const, pipeline_mode=once),
                   pl.BlockSpec((D_ATTN, D_MODEL), const, pipeline_mode=once),
                   pl.BlockSpec((D_MODEL, D_MODEL), const, pipeline_mode=once),
                   pl.BlockSpec((1, D_MODEL), const),
                   pl.BlockSpec((D_MODEL, LANES), const),
                   pl.BlockSpec((1, LANES), const)])
    return pl.pallas_call(
        functools.partial(_merge_kernel, tm=tm, seq_len=seq_len, tail=tail),
        grid=(n_i,),
        in_specs=in_specs,
        out_specs=[pl.BlockSpec((tm, D_MODEL), lambda i: (i, 0)),
                   pl.BlockSpec((tm, D_MODEL), lambda i: (i, 0)),
                   pl.BlockSpec((tm, LANES), lambda i: (i, 0)),
                   pl.BlockSpec((tm, LANES), lambda i: (i, 0)),
                   pl.BlockSpec((tail, D_CONV), lambda i: (i, 0))],
        out_shape=[jax.ShapeDtypeStruct((t, D_MODEL), F32),
                   jax.ShapeDtypeStruct((t, D_MODEL), BF16),
                   jax.ShapeDtypeStruct((t, LANES), jnp.int32),
                   jax.ShapeDtypeStruct((t, LANES), F32),
                   jax.ShapeDtypeStruct((n_i * tail, D_CONV), F32)],
        compiler_params=_cparams(("parallel",)),
        name="merge",
    )(conv3, conv3, conv3, *hist_args, ya, gates, gates, x, conv_w, wc, wa, wo, gn, wr, rb)


MOE_TM = 256
MOE_TF = 1024
MOE_TN = 1024


def _expert_changed(be_ref, i):
    return (i == 0) | (be_ref[i] != be_ref[jnp.maximum(i - 1, 0)])


def _moe_up_kernel(be_ref, nu_ref, x_ref, wg_ref, wl_ref, bg_ref, bl_ref, o_ref, wgb_ref, wlb_ref):
    i = pl.program_id(1)

    @pl.when(_expert_changed(be_ref, i))
    def _():
        wgb_ref[...] = wg_ref[0].astype(BF16)
        wlb_ref[...] = wl_ref[0].astype(BF16)

    @pl.when(i < nu_ref[0])
    def _():
        x = x_ref[...]
        h_glu = jnp.minimum(_dot(x, wgb_ref[...]) + bg_ref[0], SWIGLU_LIMIT)
        h_lin = jnp.clip(_dot(x, wlb_ref[...]) + bl_ref[0], -SWIGLU_LIMIT, SWIGLU_LIMIT)
        o_ref[...] = (h_glu * jax.nn.sigmoid(SWIGLU_ALPHA * h_glu) * (h_lin + 1.0)).astype(BF16)

    @pl.when(i >= nu_ref[0])
    def _():
        o_ref[...] = jnp.zeros_like(o_ref)


def _moe_down_kernel(be_ref, nu_ref, a_ref, w_ref, b_ref, o_ref, wb_ref):
    i = pl.program_id(1)

    @pl.when(_expert_changed(be_ref, i))
    def _():
        wb_ref[...] = w_ref[0].astype(BF16)

    @pl.when(i < nu_ref[0])
    def _():
        o_ref[...] = _dot(a_ref[...], wb_ref[...]) + b_ref[0]

    @pl.when(i >= nu_ref[0])
    def _():
        o_ref[...] = jnp.zeros_like(o_ref)


def _moe_experts(xs, block_expert, n_used, w1, b1, w2, b2):
    n_rows = xs.shape[0]
    n_i = n_rows // MOE_TM
    n_f = D_FF // MOE_TF
    b1r = b1.reshape(N_EXPERTS, 1, 2 * D_FF)
    b2r = b2.reshape(N_EXPERTS, 1, D_MODEL)
    act = pl.pallas_call(
        _moe_up_kernel,
        grid_spec=pltpu.PrefetchScalarGridSpec(
            num_scalar_prefetch=2,
            grid=(n_f, n_i),
            in_specs=[pl.BlockSpec((MOE_TM, D_MODEL), lambda j, i, be, nu: (i, 0)),
                      pl.BlockSpec((1, D_MODEL, MOE_TF), lambda j, i, be, nu: (be[i], 0, j)),
                      pl.BlockSpec((1, D_MODEL, MOE_TF), lambda j, i, be, nu: (be[i], 0, n_f + j)),
                      pl.BlockSpec((1, 1, MOE_TF), lambda j, i, be, nu: (be[i], 0, j)),
                      pl.BlockSpec((1, 1, MOE_TF), lambda j, i, be, nu: (be[i], 0, n_f + j))],
            out_specs=pl.BlockSpec((MOE_TM, MOE_TF), lambda j, i, be, nu: (i, j)),
            scratch_shapes=[pltpu.VMEM((D_MODEL, MOE_TF), BF16), pltpu.VMEM((D_MODEL, MOE_TF), BF16)]),
        out_shape=jax.ShapeDtypeStruct((n_rows, D_FF), BF16),
        compiler_params=_cparams(("arbitrary", "arbitrary")),
        name="moe_up",
    )(block_expert, n_used, xs, w1, w1, b1r, b1r)
    return pl.pallas_call(
        _moe_down_kernel,
        grid_spec=pltpu.PrefetchScalarGridSpec(
            num_scalar_prefetch=2,
            grid=(D_MODEL // MOE_TN, n_i),
            in_specs=[pl.BlockSpec((MOE_TM, D_FF), lambda j, i, be, nu: (i, 0)),
                      pl.BlockSpec((1, D_FF, MOE_TN), lambda j, i, be, nu: (be[i], 0, j)),
                      pl.BlockSpec((1, 1, MOE_TN), lambda j, i, be, nu: (be[i], 0, j))],
            out_specs=pl.BlockSpec((MOE_TM, MOE_TN), lambda j, i, be, nu: (i, j)),
            scratch_shapes=[pltpu.VMEM((D_FF, MOE_TN), BF16)]),
        out_shape=jax.ShapeDtypeStruct((n_rows, D_MODEL), F32),
        compiler_params=_cparams(("arbitrary", "arbitrary")),
        name="moe_down",
    )(block_expert, n_used, act, w2, b2r)


def _route(top_i):
    t = top_i.shape[0]
    n_assign = t * TOP_K
    flat_e = top_i.reshape(-1)
    experts = jnp.arange(N_EXPERTS, dtype=jnp.int32)
    order = jnp.argsort(flat_e).astype(jnp.int32)
    counts = jnp.sum((flat_e[:, None] == experts[None, :]).astype(jnp.int32), axis=0)
    offsets = jnp.cumsum(counts) - counts
    padded = (counts + MOE_TM - 1) // MOE_TM * MOE_TM
    padded_end = jnp.cumsum(padded)
    padded_off = padded_end - padded
    n_tiles = -(-n_assign // MOE_TM) + N_EXPERTS
    tile_start = jnp.arange(n_tiles, dtype=jnp.int32) * MOE_TM
    block_expert = jnp.minimum(jnp.sum((tile_start[:, None] >= padded_end[None, :]).astype(jnp.int32), axis=1),
                               N_EXPERTS - 1)
    in_group = (tile_start - padded_off[block_expert])[:, None] + jnp.arange(MOE_TM, dtype=jnp.int32)[None, :]
    valid = in_group < counts[block_expert][:, None]
    sorted_idx = jnp.clip(offsets[block_expert][:, None] + in_group, 0, n_assign - 1)
    row_tok = jnp.where(valid, order[sorted_idx] // TOP_K, t).reshape(-1)
    sorted_e = flat_e[order]
    dest = padded_off[sorted_e] + (jnp.arange(n_assign, dtype=jnp.int32) - offsets[sorted_e])
    _, pos = lax.sort((order, dest), num_keys=1)
    n_used = (padded_end[-1:] // MOE_TM).astype(jnp.int32)
    return row_tok, pos.reshape(t, TOP_K), block_expert, n_used


def _combine_kernel(h_ref, ys_ref, g_ref, gn_ref, o_ref):
    y = h_ref[...]
    g = g_ref[...]
    for k in range(TOP_K):
        y = y + g[:, k:k + 1] * ys_ref[k]
    inv = lax.rsqrt(jnp.mean(y * y, axis=-1, keepdims=True) + RMS_EPS)
    o_ref[...] = y * inv * gn_ref[...]


def _combine(h, ys4, g, gn, *, tm):
    t = h.shape[0]
    return pl.pallas_call(
        _combine_kernel,
        grid=(t // tm,),
        in_specs=[pl.BlockSpec((tm, D_MODEL), lambda i: (i, 0)),
                  pl.BlockSpec((TOP_K, tm, D_MODEL), lambda i: (0, i, 0)),
                  pl.BlockSpec((tm, LANES), lambda i: (i, 0)),
                  pl.BlockSpec((1, D_MODEL), lambda i: (0, 0))],
        out_specs=pl.BlockSpec((tm, D_MODEL), lambda i: (i, 0)),
        out_shape=jax.ShapeDtypeStruct((t, D_MODEL), F32),
        compiler_params=_cparams(("parallel",)),
        name="combine",
    )(h, ys4, g, gn)


def kernel(x_prompt, x_sample, cache_k, cache_v, state_conv, page_table, norm_mix, w_in, conv_w, sb_bias,
           w_br_conv, w_br_attn, w_o, norm_ffn, router_w, router_b, w_mlp1, b_mlp1, w_mlp2, b_mlp2, norm_final):
    assert w_in.shape[0] == 1, "single layer"
    batch, seq, d = x_prompt.shape
    dec_b, dec_t, _ = x_sample.shape
    n_pool, page = cache_k.shape[1], cache_k.shape[2]
    t_p, t_s = batch * seq, dec_b * dec_t
    assert dec_t > CONV_W - 1

    g_mix = norm_mix[0].reshape(1, d)
    g_ffn = norm_ffn[0].reshape(1, d)
    g_fin = norm_final.reshape(1, d)
    cw = jnp.pad(conv_w[0], ((0, SUBLANES - CONV_W), (0, 0)))
    wr = jnp.pad(router_w[0], ((0, 0), (0, LANES - N_EXPERTS)))
    rb = jnp.pad(router_b[0].astype(F32), (0, LANES - N_EXPERTS), constant_values=NEG_BIG).reshape(1, LANES)

    w_in_b = w_in[0].astype(BF16)
    weights = (w_br_conv[0].astype(BF16), w_br_attn[0].astype(BF16), w_o[0].astype(BF16), g_ffn,
               wr.astype(BF16), rb)

    xp = x_prompt.reshape(t_p, d)
    conv3, q, k, v, gates = _inproj(xp, g_mix, w_in_b, tm=1024)
    ya = _attn_prompt(q, k, v, sb_bias[0], batch, seq)
    h_p, hn_p, ti_p, tg_p, ut_p = _merge(conv3, None, ya, gates, xp, cw, *weights, tm=256, seq_len=seq)

    xs = x_sample.reshape(t_s, d)
    conv3_s, q_s, k_s, v_s, gates_s = _inproj(xs, g_mix, w_in_b, tm=t_s)
    new_rows = (dec_b, dec_t, N_HEADS, HEAD_DIM)
    ya_s = _attn_sample(q_s.reshape(new_rows), k_s.reshape(new_rows), v_s.reshape(new_rows),
                        cache_k[0], cache_v[0], page_table, sb_bias[0])
    hist = state_conv[0].astype(F32)
    zero = jnp.zeros((dec_b, dec_t - 2, D_CONV), F32)
    hist1 = jnp.concatenate([hist[:, 1:2], jnp.zeros((dec_b, dec_t - 1, D_CONV), F32)], axis=1)
    hist2 = jnp.concatenate([hist, zero], axis=1)
    h_s, hn_s, ti_s, tg_s, u_s = _merge(
        conv3_s, (hist1.reshape(t_s, D_CONV), hist2.reshape(t_s, D_CONV)), ya_s.reshape(t_s, D_ATTN),
        gates_s, xs, cw, *weights, tm=t_s, seq_len=dec_t)

    t_all = t_p + t_s
    top_i = jnp.concatenate([ti_p[:, :TOP_K], ti_s[:, :TOP_K]], axis=0)
    row_tok, pos, block_expert, n_used = _route(top_i)
    hn_all = jnp.concatenate([hn_p, hn_s, jnp.zeros((1, d), BF16)], axis=0)
    ys = _moe_experts(hn_all[row_tok], block_expert, n_used, w_mlp1[0], b_mlp1[0], w_mlp2[0], b_mlp2[0])
    ys4_p = ys[pos[:t_p].T]
    ys4_s = ys[pos[t_p:].T]
    y_p = _combine(h_p, ys4_p, tg_p, g_fin, tm=256)
    y_s = _combine(h_s, ys4_s, tg_s, g_fin, tm=t_s)

    n_hist = CONV_W - 1
    conv_p = ut_p.reshape(batch, seq // 256, SUBLANES, D_CONV)[:, -1, SUBLANES - n_hist:]
    conv_s = u_s.reshape(dec_b, dec_t, D_CONV)[:, dec_t - n_hist:]
    return (y_p.reshape(batch, seq, d), y_s.reshape(dec_b, dec_t, d),
            k.reshape(1, batch, seq, N_HEADS, HEAD_DIM), v.reshape(1, batch, seq, N_HEADS, HEAD_DIM),
            conv_p[None], k_s.reshape(1, dec_b, dec_t, N_HEADS, HEAD_DIM),
            v_s.reshape(1, dec_b, dec_t, N_HEADS, HEAD_DIM), conv_s[None])
```

```python
import functools

import jax
import jax.numpy as jnp
from jax import lax
from jax.experimental import pallas as pl
from jax.experimental.pallas import tpu as pltpu

F32 = jnp.float32
BF16 = jnp.bfloat16

D_MODEL = 2048
N_HEADS = 16
HEAD_DIM = 64
D_ATTN = N_HEADS * HEAD_DIM
D_CONV = D_MODEL // 2
CONV_W = 3
N_EXPERTS = 32
TOP_K = 4
D_FF = D_MODEL
SWIGLU_LIMIT = 7.0
SWIGLU_ALPHA = 1.702
RMS_EPS = 1e-5
D_IN = 3 * D_CONV + 3 * D_ATTN + 2 * D_MODEL

LANES = 128
SUBLANES = 8
VMEM_LIMIT = 56 * 1024 * 1024
NEG_BIG = -3.0e38


def _cparams(sem):
    return pltpu.CompilerParams(dimension_semantics=sem, vmem_limit_bytes=VMEM_LIMIT)


def _dot(a, b):
    return jnp.dot(a.astype(BF16), b.astype(BF16), preferred_element_type=F32)


IN_TN = 512
_IN_GROUPS = ((0, 3 * D_CONV // IN_TN),
              (3 * D_CONV // IN_TN, D_ATTN // IN_TN),
              ((3 * D_CONV + D_ATTN) // IN_TN, D_ATTN // IN_TN),
              ((3 * D_CONV + 2 * D_ATTN) // IN_TN, D_ATTN // IN_TN),
              ((3 * D_CONV + 3 * D_ATTN) // IN_TN, 2 * D_MODEL // IN_TN))


def _inproj_kernel(x_ref, g_ref, w_ref, *refs):
    out_refs, xn_ref = refs[:-1], refs[-1]
    j = pl.program_id(1)

    @pl.when(j == 0)
    def _():
        x = x_ref[...]
        inv = lax.rsqrt(jnp.mean(x * x, axis=-1, keepdims=True) + RMS_EPS)
        xn_ref[...] = (x * inv * g_ref[...]).astype(xn_ref.dtype)

    acc = _dot(xn_ref[...], w_ref[...])
    for o_ref, (a, n) in zip(out_refs, _IN_GROUPS):
        @pl.when((j >= a) & (j < a + n))
        def _(o_ref=o_ref):
            o_ref[...] = acc


def _inproj(x, g, w, *, tm):
    t = x.shape[0]
    n_j = D_IN // IN_TN

    def omap(a, n):
        return lambda i, j: (i, jnp.clip(j - a, 0, n - 1))

    return pl.pallas_call(
        _inproj_kernel,
        grid=(t // tm, n_j),
        in_specs=[pl.BlockSpec((tm, D_MODEL), lambda i, j: (i, 0)),
                  pl.BlockSpec((1, D_MODEL), lambda i, j: (0, 0)),
                  pl.BlockSpec((D_MODEL, IN_TN), lambda i, j: (0, j))],
        out_specs=[pl.BlockSpec((tm, IN_TN), omap(a, n)) for a, n in _IN_GROUPS],
        out_shape=[jax.ShapeDtypeStruct((t, n * IN_TN), F32) for _, n in _IN_GROUPS],
        scratch_shapes=[pltpu.VMEM((tm, D_MODEL), BF16)],
        compiler_params=_cparams(("parallel", "arbitrary")),
        name="inproj",
    )(x, g, w)


LOG2E = 1.4426950408889634
SP_LINEAR_ABOVE = 64.0


def _suffix_matrix(tk):
    m = lax.broadcasted_iota(jnp.int32, (2 * tk, tk), 0) % tk
    j = lax.broadcasted_iota(jnp.int32, (2 * tk, tk), 1)
    return (m > j).astype(BF16)


def _sb_scores(z, u2, mask):
    sp = jnp.where(z > SP_LINEAR_ABOVE, z, jnp.log(1.0 + jnp.exp2(z)) * LOG2E)
    log_beta = z - sp
    if mask is not None:
        sp = jnp.where(mask, sp, 0.0)
    hi = pltpu.bitcast(pltpu.bitcast(sp, jnp.int32) & jnp.int32(-65536), F32)
    lo = sp - hi
    hl = jnp.concatenate([hi.astype(BF16), lo.astype(BF16)], axis=1)
    half = hl.shape[0] // 2
    s = jnp.concatenate([jnp.dot(hl[:half], u2, preferred_element_type=F32),
                         jnp.dot(hl[half:], u2, preferred_element_type=F32)], axis=0)
    w = jnp.exp2(log_beta - s)
    if mask is not None:
        w = jnp.where(mask, w, 0.0)
    return w.astype(BF16), s[:, 0:1] + sp[:, 0:1]


ATT_T = 256


def _attn_prompt_kernel(q_ref, k_ref, v_ref, bias_ref, u2_ref, o_ref, kb_ref, vb_ref, acc_ref):
    qi = pl.program_id(2)
    t = ATT_T

    @pl.when(qi == 0)
    def _():
        kb_ref[...] = k_ref[...].astype(BF16)
        vb_ref[...] = v_ref[...].astype(BF16)

    q = q_ref[...] * (HEAD_DIM ** -0.5 * LOG2E)
    lane = lax.broadcasted_iota(jnp.int32, (t, LANES), 1)
    q2 = jnp.concatenate([jnp.where(lane < HEAD_DIM, q, 0.0), jnp.where(lane >= HEAD_DIM, q, 0.0)],
                         axis=0).astype(BF16)
    bias = bias_ref[0]
    u2 = u2_ref[...]

    def logits(j):
        kb = kb_ref[pl.ds(pl.multiple_of(j * t, t), t), :]
        return lax.dot_general(q2, kb, (((1,), (1,)), ((), ())), preferred_element_type=F32) + bias

    def accumulate(j, w, carry):
        vb = vb_ref[pl.ds(pl.multiple_of(j * t, t), t), :]
        acc_ref[...] += jnp.exp2(-carry) * jnp.dot(w, vb, preferred_element_type=F32)

    acc_ref[...] = jnp.zeros_like(acc_ref)
    row = lax.broadcasted_iota(jnp.int32, (2 * t, t), 0) % t
    col = lax.broadcasted_iota(jnp.int32, (2 * t, t), 1)
    w, tot = _sb_scores(logits(qi), u2, col < row)
    z = logits(jnp.maximum(qi - 1, 0))

    def body(n, c):
        z, w, tot, carry = c
        z_next = logits(jnp.maximum(qi - n - 2, 0))
        nxt = _sb_scores(z, u2, None)
        accumulate(qi - n, w, carry)
        return (z_next,) + nxt + (carry + tot,)

    _, w, _, carry = lax.fori_loop(0, qi, body, (z, w, tot, jnp.zeros((2 * t, 1), F32)))
    accumulate(0, w, carry)
    acc = acc_ref[...]
    o_ref[...] = jnp.where(lane < HEAD_DIM, acc[:t], acc[t:])


def _attn_prompt(q, k, v, sb_bias, batch, seq):
    t = ATT_T
    n_q = seq // t
    n_hp = N_HEADS // 2
    bias = jnp.broadcast_to(jnp.repeat(sb_bias.astype(F32).reshape(n_hp, 2) * LOG2E, t, axis=1)[:, :, None],
                            (n_hp, 2 * t, t))
    return pl.pallas_call(
        _attn_prompt_kernel,
        grid=(batch, n_hp, n_q),
        in_specs=[pl.BlockSpec((t, LANES), lambda b, h, i: (b * n_q + i, h)),
                  pl.BlockSpec((seq, LANES), lambda b, h, i: (b, h)),
                  pl.BlockSpec((seq, LANES), lambda b, h, i: (b, h)),
                  pl.BlockSpec((1, 2 * t, t), lambda b, h, i: (h, 0, 0)),
                  pl.BlockSpec((2 * t, t), lambda b, h, i: (0, 0))],
        out_specs=pl.BlockSpec((t, LANES), lambda b, h, i: (b * n_q + i, h)),
        out_shape=jax.ShapeDtypeStruct((batch * seq, D_ATTN), F32),
        scratch_shapes=[pltpu.VMEM((seq, LANES), BF16), pltpu.VMEM((seq, LANES), BF16),
                        pltpu.VMEM((2 * t, LANES), F32)],
        compiler_params=_cparams(("parallel", "parallel", "arbitrary")),
        name="attn_prompt",
    )(q, k, v, bias, _suffix_matrix(t))


SAMPLE_QR = 16
SAMPLE_PAGES_PER_STEP = 8


def _attn_sample_kernel(pt_ref, q_ref, kn_ref, vn_ref, *refs, page, pages_per_step):
    kc_refs, vc_refs = refs[:pages_per_step], refs[pages_per_step:2 * pages_per_step]
    bias_ref, u2_ref, o_ref, qb_ref, acc_ref, carry_ref = refs[2 * pages_per_step:]
    p = pl.program_id(1)
    qr = SAMPLE_QR
    u2 = u2_ref[...]

    def head_slab(ref, h):
        return ref[h * HEAD_DIM:(h + 1) * HEAD_DIM, :].astype(BF16)

    def scores(k_ref, mask):
        z = jnp.concatenate(
            [jnp.dot(qb_ref[h * qr:(h + 1) * qr, :], head_slab(k_ref, h), preferred_element_type=F32)
             for h in range(N_HEADS)], axis=0) * (HEAD_DIM ** -0.5 * LOG2E) + bias_ref[...]
        return _sb_scores(z, u2, mask)

    def sweep(k_refs, v_refs, mask):
        blocks = [scores(k_ref, mask) for k_ref in k_refs]
        carry = carry_ref[...]
        acc = [acc_ref[h * qr:(h + 1) * qr, :] for h in range(N_HEADS)]
        for (w, tot), v_ref in zip(blocks, v_refs):
            scale = jnp.exp2(-carry)
            carry = carry + tot
            for h in range(N_HEADS):
                rows = slice(h * qr, (h + 1) * qr)
                acc[h] = acc[h] + scale[rows] * lax.dot_general(
                    w[rows], head_slab(v_ref, h), (((1,), (1,)), ((), ())), preferred_element_type=F32)
        carry_ref[...] = carry
        for h in range(N_HEADS):
            acc_ref[h * qr:(h + 1) * qr, :] = acc[h]

    @pl.when(p == 0)
    def _():
        qb_ref[...] = q_ref[...].astype(BF16)
        carry_ref[...] = jnp.zeros_like(carry_ref)
        acc_ref[...] = jnp.zeros_like(acc_ref)
        qpos = lax.broadcasted_iota(jnp.int32, (N_HEADS * qr, page), 0) % qr
        kpos = lax.broadcasted_iota(jnp.int32, (N_HEADS * qr, page), 1)
        sweep([kn_ref], [vn_ref], kpos < qpos)

    sweep(kc_refs, vc_refs, None)

    @pl.when(p == pl.num_programs(1) - 1)
    def _():
        o_ref[...] = acc_ref[...]


def _attn_sample(q, k, v, cache_k, cache_v, page_table, sb_bias):
    dec_b, dec_t = q.shape[:2]
    n_pool, page = cache_k.shape[:2]
    n_pages = page_table.shape[1]
    pps = SAMPLE_PAGES_PER_STEP
    qr = SAMPLE_QR
    assert page == LANES and dec_t <= qr and n_pages % pps == 0
    rows = N_HEADS * qr
    q_rows = jnp.pad(q.transpose(0, 2, 1, 3), ((0, 0), (0, 0), (0, qr - dec_t), (0, 0))).reshape(dec_b, rows, HEAD_DIM)
    pad = ((0, 0), (0, 0), (0, 0), (0, page - dec_t))
    k_new = jnp.pad(k.transpose(0, 2, 3, 1), pad).reshape(dec_b, D_ATTN, page)
    v_new = jnp.pad(v.transpose(0, 2, 3, 1), pad).reshape(dec_b, D_ATTN, page)
    bias = jnp.broadcast_to(jnp.repeat(sb_bias.astype(F32) * LOG2E, qr)[:, None], (rows, page))
    pool_t = lambda c: c.transpose(0, 2, 3, 1).reshape(n_pool, D_ATTN, page)
    page_block = (None, D_ATTN, page)

    def page_spec(s):
        return pl.BlockSpec(page_block, lambda b, p, pt: (pt[b * n_pages + n_pages - 1 - (p * pps + s)], 0, 0))

    per_seq = lambda b, p, pt: (b, 0, 0)
    out = pl.pallas_call(
        functools.partial(_attn_sample_kernel, page=page, pages_per_step=pps),
        grid_spec=pltpu.PrefetchScalarGridSpec(
            num_scalar_prefetch=1,
            grid=(dec_b, n_pages // pps),
            in_specs=([pl.BlockSpec((None, rows, HEAD_DIM), per_seq),
                       pl.BlockSpec(page_block, per_seq),
                       pl.BlockSpec(page_block, per_seq)]
                      + [page_spec(s) for s in range(pps)] * 2
                      + [pl.BlockSpec((rows, page), lambda b, p, pt: (0, 0)),
                         pl.BlockSpec((2 * page, page), lambda b, p, pt: (0, 0))]),
            out_specs=pl.BlockSpec((None, rows, HEAD_DIM), per_seq),
            scratch_shapes=[pltpu.VMEM((rows, HEAD_DIM), BF16), pltpu.VMEM((rows, HEAD_DIM), F32),
                            pltpu.VMEM((rows, 1), F32)]),
        out_shape=jax.ShapeDtypeStruct((dec_b, rows, HEAD_DIM), F32),
        compiler_params=_cparams(("parallel", "arbitrary")),
        name="attn_sample",
    )(page_table.reshape(-1), q_rows, k_new, v_new,
      *([pool_t(cache_k)] * pps), *([pool_t(cache_v)] * pps), bias, _suffix_matrix(page))
    return out.reshape(dec_b, N_HEADS, qr, HEAD_DIM)[:, :, :dec_t].transpose(0, 2, 1, 3)


def _top4(logits):
    lane = lax.broadcasted_iota(jnp.int32, logits.shape, 1)
    vals, idxs = [], []
    l = logits
    for _ in range(TOP_K):
        m = jnp.max(l, axis=-1, keepdims=True)
        idx = jnp.min(jnp.where(l == m, lane, LANES), axis=-1, keepdims=True)
        vals.append(m)
        idxs.append(idx)
        l = jnp.where(lane == idx, NEG_BIG, l)
    es = [jnp.exp(v - vals[0]) for v in vals]
    inv = 1.0 / (es[0] + es[1] + es[2] + es[3])
    out_i = jnp.zeros(logits.shape, jnp.int32)
    out_g = jnp.zeros(logits.shape, F32)
    for k in range(TOP_K):
        out_i = jnp.where(lane == k, idxs[k], out_i)
        out_g = jnp.where(lane == k, es[k] * inv, out_g)
    return out_i, out_g


def _merge_kernel(*refs, tm, seq_len, tail):
    if seq_len >= tm:
        cb_ref, cc_ref, ch_ref, pcc_ref, pch_ref = refs[:5]
    else:
        cb_ref, cc_ref, ch_ref, h1_ref, h2_ref = refs[:5]
    (ya_ref, ga_ref, gb_ref, x_ref, cw_ref, wc_ref, wa_ref, wo_ref, gn_ref, wr_ref, rb_ref,
     h_ref, hn_ref, ti_ref, tg_ref, ut_ref) = refs[5:]
    i = pl.program_id(0)

    u = cc_ref[...] * ch_ref[...]
    row = lax.broadcasted_iota(jnp.int32, u.shape, 0)
    if seq_len >= tm:
        at_start = (i % (seq_len // tm)) == 0
        up = jnp.where(at_start, 0.0, pcc_ref[...] * pch_ref[...])
        u1 = jnp.where(row == 0, up[SUBLANES - 1:SUBLANES], pltpu.roll(u, 1, axis=0))
        u2 = jnp.where(row == 0, up[SUBLANES - 2:SUBLANES - 1],
                       jnp.where(row == 1, up[SUBLANES - 1:SUBLANES], pltpu.roll(u, 2, axis=0)))
    else:
        u1 = jnp.where(row % seq_len == 0, h1_ref[...], pltpu.roll(u, 1, axis=0))
        u2 = jnp.where(row % seq_len < 2, h2_ref[...], pltpu.roll(u, 2, axis=0))
    ut_ref[...] = u[tm - tail:]
    cw = cw_ref[...]
    y_conv = cb_ref[...] * (cw[0:1] * u2 + cw[1:2] * u1 + cw[2:3] * u)

    merged = (jax.nn.sigmoid(ga_ref[...]) * _dot(y_conv, wc_ref[...])
              + jax.nn.sigmoid(gb_ref[...]) * _dot(ya_ref[...], wa_ref[...]))
    h = x_ref[...] + _dot(merged, wo_ref[...])
    h_ref[...] = h
    inv = lax.rsqrt(jnp.mean(h * h, axis=-1, keepdims=True) + RMS_EPS)
    hn = h * inv * gn_ref[...]
    hn_ref[...] = hn.astype(BF16)
    logits = _dot(hn, wr_ref[...]) + rb_ref[...]
    ti, tg = _top4(logits)
    ti_ref[...] = ti
    tg_ref[...] = tg


def _merge(conv3, prev_or_hist, ya, gates, x, conv_w, wc, wa, wo, gn, wr, rb, *, tm, seq_len):
    t = x.shape[0]
    n_i = t // tm
    tail = min(tm, SUBLANES) if seq_len >= tm else tm
    const = lambda i: (0, 0)
    once = pl.Buffered(1)
    if seq_len >= tm:
        r = tm // SUBLANES
        prev_map_c = lambda i: (jnp.maximum(i * r - 1, 0), 1)
        prev_map_h = lambda i: (jnp.maximum(i * r - 1, 0), 2)
        hist_specs = [pl.BlockSpec((SUBLANES, D_CONV), prev_map_c), pl.BlockSpec((SUBLANES, D_CONV), prev_map_h)]
        hist_args = (conv3, conv3)
    else:
        hist_specs = [pl.BlockSpec((tm, D_CONV), lambda i: (i, 0))] * 2
        hist_args = prev_or_hist
    in_specs = ([pl.BlockSpec((tm, D_CONV), lambda i: (i, 0)),
                 pl.BlockSpec((tm, D_CONV), lambda i: (i, 1)),
                 pl.BlockSpec((tm, D_CONV), lambda i: (i, 2))]
                + hist_specs
                + [pl.BlockSpec((tm, D_ATTN), lambda i: (i, 0)),
                   pl.BlockSpec((tm, D_MODEL), lambda i: (i, 0)),
                   pl.BlockSpec((tm, D_MODEL), lambda i: (i, 1)),
                   pl.BlockSpec((tm, D_MODEL), lambda i: (i, 0)),
                   pl.BlockSpec((SUBLANES, D_CONV), const),
                   pl.BlockSpec((D_CONV, D_MODEL), const, pipeline_mode=once),
                   pl.BlockSpec((D_ATTN, D_MODEL), const, pipeline_mode=once),
                   pl.BlockSpec((D_MODEL, D_MODEL), const, pipeline_mode=once),
                   pl.BlockSpec((1, D_MODEL), const),
                   pl.BlockSpec((D_MODEL, LANES), const),
                   pl.BlockSpec((1, LANES), const)])
    return pl.pallas_call(
        functools.partial(_merge_kernel, tm=tm, seq_len=seq_len, tail=tail),
        grid=(n_i,),
        in_specs=in_specs,
        out_specs=[pl.BlockSpec((tm, D_MODEL), lambda i: (i, 0)),
                   pl.BlockSpec((tm, D_MODEL), lambda i: (i, 0)),
                   pl.BlockSpec((tm, LANES), lambda i: (i, 0)),
                   pl.BlockSpec((tm, LANES), lambda i: (i, 0)),
                   pl.BlockSpec((tail, D_CONV), lambda i: (i, 0))],
        out_shape=[jax.ShapeDtypeStruct((t, D_MODEL), F32),
                   jax.ShapeDtypeStruct((t, D_MODEL), BF16),
                   jax.ShapeDtypeStruct((t, LANES), jnp.int32),
                   jax.ShapeDtypeStruct((t, LANES), F32),
                   jax.ShapeDtypeStruct((n_i * tail, D_CONV), F32)],
        compiler_params=_cparams(("parallel",)),
        name="merge",
    )(conv3, conv3, conv3, *hist_args, ya, gates, gates, x, conv_w, wc, wa, wo, gn, wr, rb)


MOE_TM = 256
MOE_TF = 1024
MOE_TN = 1024


def _expert_changed(be_ref, i):
    return (i == 0) | (be_ref[i] != be_ref[jnp.maximum(i - 1, 0)])


def _moe_up_kernel(be_ref, nu_ref, x_ref, wg_ref, wl_ref, bg_ref, bl_ref, o_ref, wgb_ref, wlb_ref):
    i = pl.program_id(1)

    @pl.when(_expert_changed(be_ref, i))
    def _():
        wgb_ref[...] = wg_ref[0].astype(BF16)
        wlb_ref[...] = wl_ref[0].astype(BF16)

    @pl.when(i < nu_ref[0])
    def _():
        x = x_ref[...]
        h_glu = jnp.minimum(_dot(x, wgb_ref[...]) + bg_ref[0], SWIGLU_LIMIT)
        h_lin = jnp.clip(_dot(x, wlb_ref[...]) + bl_ref[0], -SWIGLU_LIMIT, SWIGLU_LIMIT)
        o_ref[...] = (h_glu * jax.nn.sigmoid(SWIGLU_ALPHA * h_glu) * (h_lin + 1.0)).astype(BF16)

    @pl.when(i >= nu_ref[0])
    def _():
        o_ref[...] = jnp.zeros_like(o_ref)


def _moe_down_kernel(be_ref, nu_ref, a_ref, w_ref, b_ref, o_ref, wb_ref):
    i = pl.program_id(1)

    @pl.when(_expert_changed(be_ref, i))
    def _():
        wb_ref[...] = w_ref[0].astype(BF16)

    @pl.when(i < nu_ref[0])
    def _():
        o_ref[...] = _dot(a_ref[...], wb_ref[...]) + b_ref[0]

    @pl.when(i >= nu_ref[0])
    def _():
        o_ref[...] = jnp.zeros_like(o_ref)


def _moe_experts(xs, block_expert, n_used, w1, b1, w2, b2):
    n_rows = xs.shape[0]
    n_i = n_rows // MOE_TM
    n_f = D_FF // MOE_TF
    b1r = b1.reshape(N_EXPERTS, 1, 2 * D_FF)
    b2r = b2.reshape(N_EXPERTS, 1, D_MODEL)
    act = pl.pallas_call(
        _moe_up_kernel,
        grid_spec=pltpu.PrefetchScalarGridSpec(
            num_scalar_prefetch=2,
            grid=(n_f, n_i),
            in_specs=[pl.BlockSpec((MOE_TM, D_MODEL), lambda j, i, be, nu: (i, 0)),
                      pl.BlockSpec((1, D_MODEL, MOE_TF), lambda j, i, be, nu: (be[i], 0, j)),
                      pl.BlockSpec((1, D_MODEL, MOE_TF), lambda j, i, be, nu: (be[i], 0, n_f + j)),
                      pl.BlockSpec((1, 1, MOE_TF), lambda j, i, be, nu: (be[i], 0, j)),
                      pl.BlockSpec((1, 1, MOE_TF), lambda j, i, be, nu: (be[i], 0, n_f + j))],
            out_specs=pl.BlockSpec((MOE_TM, MOE_TF), lambda j, i, be, nu: (i, j)),
            scratch_shapes=[pltpu.VMEM((D_MODEL, MOE_TF), BF16), pltpu.VMEM((D_MODEL, MOE_TF), BF16)]),
        out_shape=jax.ShapeDtypeStruct((n_rows, D_FF), BF16),
        compiler_params=_cparams(("arbitrary", "arbitrary")),
        name="moe_up",
    )(block_expert, n_used, xs, w1, w1, b1r, b1r)
    return pl.pallas_call(
        _moe_down_kernel,
        grid_spec=pltpu.PrefetchScalarGridSpec(
            num_scalar_prefetch=2,
            grid=(D_MODEL // MOE_TN, n_i),
            in_specs=[pl.BlockSpec((MOE_TM, D_FF), lambda j, i, be, nu: (i, 0)),
                      pl.BlockSpec((1, D_FF, MOE_TN), lambda j, i, be, nu: (be[i], 0, j)),
                      pl.BlockSpec((1, 1, MOE_TN), lambda j, i, be, nu: (be[i], 0, j))],
            out_specs=pl.BlockSpec((MOE_TM, MOE_TN), lambda j, i, be, nu: (i, j)),
            scratch_shapes=[pltpu.VMEM((D_FF, MOE_TN), BF16)]),
        out_shape=jax.ShapeDtypeStruct((n_rows, D_MODEL), F32),
        compiler_params=_cparams(("arbitrary", "arbitrary")),
        name="moe_down",
    )(block_expert, n_used, act, w2, b2r)


def _route(top_i):
    t = top_i.shape[0]
    n_assign = t * TOP_K
    flat_e = top_i.reshape(-1)
    experts = jnp.arange(N_EXPERTS, dtype=jnp.int32)
    order = jnp.argsort(flat_e).astype(jnp.int32)
    counts = jnp.sum((flat_e[:, None] == experts[None, :]).astype(jnp.int32), axis=0)
    offsets = jnp.cumsum(counts) - counts
    padded = (counts + MOE_TM - 1) // MOE_TM * MOE_TM
    padded_end = jnp.cumsum(padded)
    padded_off = padded_end - padded
    n_tiles = -(-n_assign // MOE_TM) + N_EXPERTS
    tile_start = jnp.arange(n_tiles, dtype=jnp.int32) * MOE_TM
    block_expert = jnp.minimum(jnp.sum((tile_start[:, None] >= padded_end[None, :]).astype(jnp.int32), axis=1),
                               N_EXPERTS - 1)
    in_group = (tile_start - padded_off[block_expert])[:, None] + jnp.arange(MOE_TM, dtype=jnp.int32)[None, :]
    valid = in_group < counts[block_expert][:, None]
    sorted_idx = jnp.clip(offsets[block_expert][:, None] + in_group, 0, n_assign - 1)
    row_tok = jnp.where(valid, order[sorted_idx] // TOP_K, t).reshape(-1)
    sorted_e = flat_e[order]
    dest = padded_off[sorted_e] + (jnp.arange(n_assign, dtype=jnp.int32) - offsets[sorted_e])
    _, pos = lax.sort((order, dest), num_keys=1)
    n_used = (padded_end[-1:] // MOE_TM).astype(jnp.int32)
    return row_tok, pos.reshape(t, TOP_K), block_expert, n_used


COMBINE_UNROLL = 8


def _combine_kernel(pos_ref, h_ref, g_ref, gn_ref, ys_ref, o_ref, buf_ref, sem_ref, *, tm):
    i = pl.program_id(0)
    n_i = pl.num_programs(0)

    def row_copy(tile, slot, r, k):
        src = pos_ref[(tile * tm + r) * TOP_K + k]
        return pltpu.make_async_copy(ys_ref.at[pl.ds(src, 1)], buf_ref.at[slot, k, pl.ds(r, 1)], sem_ref.at[slot])

    def for_rows(tile, slot, fn):
        def body(r, c):
            for k in range(TOP_K):
                fn(row_copy(tile, slot, r, k))
            return c
        lax.fori_loop(0, tm, body, 0, unroll=COMBINE_UNROLL)

    slot = i % 2

    @pl.when(i == 0)
    def _():
        for_rows(0, 0, lambda c: c.start())

    @pl.when(i + 1 < n_i)
    def _():
        for_rows(i + 1, 1 - slot, lambda c: c.start())

    for_rows(i, slot, lambda c: c.wait())
    y = h_ref[...]
    g = g_ref[...]
    for k in range(TOP_K):
        y = y + g[:, k:k + 1] * buf_ref[slot, k]
    inv = lax.rsqrt(jnp.mean(y * y, axis=-1, keepdims=True) + RMS_EPS)
    o_ref[...] = y * inv * gn_ref[...]


def _combine(h, ys, pos, g, gn, *, tm):
    t = h.shape[0]
    return pl.pallas_call(
        functools.partial(_combine_kernel, tm=tm),
        grid_spec=pltpu.PrefetchScalarGridSpec(
            num_scalar_prefetch=1,
            grid=(t // tm,),
            in_specs=[pl.BlockSpec((tm, D_MODEL), lambda i, pos: (i, 0)),
                      pl.BlockSpec((tm, LANES), lambda i, pos: (i, 0)),
                      pl.BlockSpec((1, D_MODEL), lambda i, pos: (0, 0)),
                      pl.BlockSpec(memory_space=pl.ANY)],
            out_specs=pl.BlockSpec((tm, D_MODEL), lambda i, pos: (i, 0)),
            scratch_shapes=[pltpu.VMEM((2, TOP_K, tm, D_MODEL), F32), pltpu.SemaphoreType.DMA((2,))]),
        out_shape=jax.ShapeDtypeStruct((t, D_MODEL), F32),
        compiler_params=_cparams(("arbitrary",)),
        name="combine",
    )(pos.reshape(-1), h, g, gn, ys)


def kernel(x_prompt, x_sample, cache_k, cache_v, state_conv, page_table, norm_mix, w_in, conv_w, sb_bias,
           w_br_conv, w_br_attn, w_o, norm_ffn, router_w, router_b, w_mlp1, b_mlp1, w_mlp2, b_mlp2, norm_final):
    assert w_in.shape[0] == 1, "single layer"
    batch, seq, d = x_prompt.shape
    dec_b, dec_t, _ = x_sample.shape
    n_pool, page = cache_k.shape[1], cache_k.shape[2]
    t_p, t_s = batch * seq, dec_b * dec_t
    assert dec_t > CONV_W - 1

    g_mix = norm_mix[0].reshape(1, d)
    g_ffn = norm_ffn[0].reshape(1, d)
    g_fin = norm_final.reshape(1, d)
    cw = jnp.pad(conv_w[0], ((0, SUBLANES - CONV_W), (0, 0)))
    wr = jnp.pad(router_w[0], ((0, 0), (0, LANES - N_EXPERTS)))
    rb = jnp.pad(router_b[0].astype(F32), (0, LANES - N_EXPERTS), constant_values=NEG_BIG).reshape(1, LANES)

    w_in_b = w_in[0].astype(BF16)
    weights = (w_br_conv[0].astype(BF16), w_br_attn[0].astype(BF16), w_o[0].astype(BF16), g_ffn,
               wr.astype(BF16), rb)

    xp = x_prompt.reshape(t_p, d)
    conv3, q, k, v, gates = _inproj(xp, g_mix, w_in_b, tm=1024)
    ya = _attn_prompt(q, k, v, sb_bias[0], batch, seq)
    h_p, hn_p, ti_p, tg_p, ut_p = _merge(conv3, None, ya, gates, xp, cw, *weights, tm=256, seq_len=seq)

    xs = x_sample.reshape(t_s, d)
    conv3_s, q_s, k_s, v_s, gates_s = _inproj(xs, g_mix, w_in_b, tm=t_s)
    new_rows = (dec_b, dec_t, N_HEADS, HEAD_DIM)
    ya_s = _attn_sample(q_s.reshape(new_rows), k_s.reshape(new_rows), v_s.reshape(new_rows),
                        cache_k[0], cache_v[0], page_table, sb_bias[0])
    hist = state_conv[0].astype(F32)
    zero = jnp.zeros((dec_b, dec_t - 2, D_CONV), F32)
    hist1 = jnp.concatenate([hist[:, 1:2], jnp.zeros((dec_b, dec_t - 1, D_CONV), F32)], axis=1)
    hist2 = jnp.concatenate([hist, zero], axis=1)
    h_s, hn_s, ti_s, tg_s, u_s = _merge(
        conv3_s, (hist1.reshape(t_s, D_CONV), hist2.reshape(t_s, D_CONV)), ya_s.reshape(t_s, D_ATTN),
        gates_s, xs, cw, *weights, tm=t_s, seq_len=dec_t)

    t_all = t_p + t_s
    top_i = jnp.concatenate([ti_p[:, :TOP_K], ti_s[:, :TOP_K]], axis=0)
    row_tok, pos, block_expert, n_used = _route(top_i)
    hn_all = jnp.concatenate([hn_p, hn_s, jnp.zeros((1, d), BF16)], axis=0)
    ys = _moe_experts(hn_all[row_tok], block_expert, n_used, w_mlp1[0], b_mlp1[0], w_mlp2[0], b_mlp2[0])
    y_p = _combine(h_p, ys, pos[:t_p], tg_p, g_fin, tm=128)
    y_s = _combine(h_s, ys, pos[t_p:], tg_s, g_fin, tm=t_s)

    n_hist = CONV_W - 1
    conv_p = ut_p.reshape(batch, seq // 256, SUBLANES, D_CONV)[:, -1, SUBLANES - n_hist:]
    conv_s = u_s.reshape(dec_b, dec_t, D_CONV)[:, dec_t - n_hist:]
    return (y_p.reshape(batch, seq, d), y_s.reshape(dec_b, dec_t, d),
            k.reshape(1, batch, seq, N_HEADS, HEAD_DIM), v.reshape(1, batch, seq, N_HEADS, HEAD_DIM),
            conv_p[None], k_s.reshape(1, dec_b, dec_t, N_HEADS, HEAD_DIM),
            v_s.reshape(1, dec_b, dec_t, N_HEADS, HEAD_DIM), conv_s[None])
```

```python
import functools

import jax
import jax.numpy as jnp
from jax import lax
from jax.experimental import pallas as pl
from jax.experimental.pallas import tpu as pltpu

F32 = jnp.float32
BF16 = jnp.bfloat16

D_MODEL = 2048
N_HEADS = 16
HEAD_DIM = 64
D_ATTN = N_HEADS * HEAD_DIM
D_CONV = D_MODEL // 2
CONV_W = 3
N_EXPERTS = 32
TOP_K = 4
D_FF = D_MODEL
SWIGLU_LIMIT = 7.0
SWIGLU_ALPHA = 1.702
RMS_EPS = 1e-5
D_IN = 3 * D_CONV + 3 * D_ATTN + 2 * D_MODEL

LANES = 128
SUBLANES = 8
VMEM_LIMIT = 56 * 1024 * 1024
NEG_BIG = -3.0e38


def _cparams(sem):
    return pltpu.CompilerParams(dimension_semantics=sem, vmem_limit_bytes=VMEM_LIMIT)


def _dot(a, b):
    return jnp.dot(a.astype(BF16), b.astype(BF16), preferred_element_type=F32)


IN_TN = 512
_IN_GROUPS = ((0, 3 * D_CONV // IN_TN),
              (3 * D_CONV // IN_TN, D_ATTN // IN_TN),
              ((3 * D_CONV + D_ATTN) // IN_TN, D_ATTN // IN_TN),
              ((3 * D_CONV + 2 * D_ATTN) // IN_TN, D_ATTN // IN_TN),
              ((3 * D_CONV + 3 * D_ATTN) // IN_TN, 2 * D_MODEL // IN_TN))


def _inproj_kernel(x_ref, g_ref, w_ref, *refs):
    out_refs, xn_ref = refs[:-1], refs[-1]
    j = pl.program_id(1)

    @pl.when(j == 0)
    def _():
        x = x_ref[...]
        inv = lax.rsqrt(jnp.mean(x * x, axis=-1, keepdims=True) + RMS_EPS)
        xn_ref[...] = (x * inv * g_ref[...]).astype(xn_ref.dtype)

    acc = _dot(xn_ref[...], w_ref[...])
    for o_ref, (a, n) in zip(out_refs, _IN_GROUPS):
        @pl.when((j >= a) & (j < a + n))
        def _(o_ref=o_ref):
            o_ref[...] = acc


def _inproj(x, g, w, *, tm):
    t = x.shape[0]
    n_j = D_IN // IN_TN

    def omap(a, n):
        return lambda i, j: (i, jnp.clip(j - a, 0, n - 1))

    return pl.pallas_call(
        _inproj_kernel,
        grid=(t // tm, n_j),
        in_specs=[pl.BlockSpec((tm, D_MODEL), lambda i, j: (i, 0)),
                  pl.BlockSpec((1, D_MODEL), lambda i, j: (0, 0)),
                  pl.BlockSpec((D_MODEL, IN_TN), lambda i, j: (0, j))],
        out_specs=[pl.BlockSpec((tm, IN_TN), omap(a, n)) for a, n in _IN_GROUPS],
        out_shape=[jax.ShapeDtypeStruct((t, n * IN_TN), F32) for _, n in _IN_GROUPS],
        scratch_shapes=[pltpu.VMEM((tm, D_MODEL), BF16)],
        compiler_params=_cparams(("parallel", "arbitrary")),
        name="inproj",
    )(x, g, w)


LOG2E = 1.4426950408889634
SP_LINEAR_ABOVE = 64.0


def _suffix_matrix(tk):
    m = lax.broadcasted_iota(jnp.int32, (2 * tk, tk), 0) % tk
    j = lax.broadcasted_iota(jnp.int32, (2 * tk, tk), 1)
    return (m > j).astype(BF16)


def _sb_scores(z, u2, mask):
    sp = jnp.where(z > SP_LINEAR_ABOVE, z, jnp.log(1.0 + jnp.exp2(z)) * LOG2E)
    log_beta = z - sp
    if mask is not None:
        sp = jnp.where(mask, sp, 0.0)
    hi = pltpu.bitcast(pltpu.bitcast(sp, jnp.int32) & jnp.int32(-65536), F32)
    lo = sp - hi
    hl = jnp.concatenate([hi.astype(BF16), lo.astype(BF16)], axis=1)
    half = hl.shape[0] // 2
    s = jnp.concatenate([jnp.dot(hl[:half], u2, preferred_element_type=F32),
                         jnp.dot(hl[half:], u2, preferred_element_type=F32)], axis=0)
    w = jnp.exp2(log_beta - s)
    if mask is not None:
        w = jnp.where(mask, w, 0.0)
    return w.astype(BF16), s[:, 0:1] + sp[:, 0:1]


ATT_T = 256


def _attn_prompt_kernel(q_ref, k_ref, v_ref, bias_ref, u2_ref, o_ref, kb_ref, vb_ref, acc_ref):
    qi = pl.program_id(2)
    t = ATT_T

    @pl.when(qi == 0)
    def _():
        kb_ref[...] = k_ref[...].astype(BF16)
        vb_ref[...] = v_ref[...].astype(BF16)

    q = q_ref[...] * (HEAD_DIM ** -0.5 * LOG2E)
    lane = lax.broadcasted_iota(jnp.int32, (t, LANES), 1)
    q2 = jnp.concatenate([jnp.where(lane < HEAD_DIM, q, 0.0), jnp.where(lane >= HEAD_DIM, q, 0.0)],
                         axis=0).astype(BF16)
    bias = bias_ref[0]
    u2 = u2_ref[...]

    def logits(j):
        kb = kb_ref[pl.ds(pl.multiple_of(j * t, t), t), :]
        return lax.dot_general(q2, kb, (((1,), (1,)), ((), ())), preferred_element_type=F32) + bias

    def accumulate(j, w, carry):
        vb = vb_ref[pl.ds(pl.multiple_of(j * t, t), t), :]
        acc_ref[...] += jnp.exp2(-carry) * jnp.dot(w, vb, preferred_element_type=F32)

    acc_ref[...] = jnp.zeros_like(acc_ref)
    row = lax.broadcasted_iota(jnp.int32, (2 * t, t), 0) % t
    col = lax.broadcasted_iota(jnp.int32, (2 * t, t), 1)
    w, tot = _sb_scores(logits(qi), u2, col < row)
    z = logits(jnp.maximum(qi - 1, 0))

    def body(n, c):
        z, w, tot, carry = c
        z_next = logits(jnp.maximum(qi - n - 2, 0))
        nxt = _sb_scores(z, u2, None)
        accumulate(qi - n, w, carry)
        return (z_next,) + nxt + (carry + tot,)

    _, w, _, carry = lax.fori_loop(0, qi, body, (z, w, tot, jnp.zeros((2 * t, 1), F32)))
    accumulate(0, w, carry)
    acc = acc_ref[...]
    o_ref[...] = jnp.where(lane < HEAD_DIM, acc[:t], acc[t:])


def _attn_prompt(q, k, v, sb_bias, batch, seq):
    t = ATT_T
    n_q = seq // t
    n_hp = N_HEADS // 2
    bias = jnp.broadcast_to(jnp.repeat(sb_bias.astype(F32).reshape(n_hp, 2) * LOG2E, t, axis=1)[:, :, None],
                            (n_hp, 2 * t, t))
    return pl.pallas_call(
        _attn_prompt_kernel,
        grid=(batch, n_hp, n_q),
        in_specs=[pl.BlockSpec((t, LANES), lambda b, h, i: (b * n_q + i, h)),
                  pl.BlockSpec((seq, LANES), lambda b, h, i: (b, h)),
                  pl.BlockSpec((seq, LANES), lambda b, h, i: (b, h)),
                  pl.BlockSpec((1, 2 * t, t), lambda b, h, i: (h, 0, 0)),
                  pl.BlockSpec((2 * t, t), lambda b, h, i: (0, 0))],
        out_specs=pl.BlockSpec((t, LANES), lambda b, h, i: (b * n_q + i, h)),
        out_shape=jax.ShapeDtypeStruct((batch * seq, D_ATTN), F32),
        scratch_shapes=[pltpu.VMEM((seq, LANES), BF16), pltpu.VMEM((seq, LANES), BF16),
                        pltpu.VMEM((2 * t, LANES), F32)],
        compiler_params=_cparams(("parallel", "parallel", "arbitrary")),
        name="attn_prompt",
    )(q, k, v, bias, _suffix_matrix(t))


SAMPLE_QR = 16
SAMPLE_PAGES_PER_STEP = 8


def _attn_sample_kernel(pt_ref, q_ref, kn_ref, vn_ref, *refs, page, pages_per_step):
    kc_refs, vc_refs = refs[:pages_per_step], refs[pages_per_step:2 * pages_per_step]
    bias_ref, u2_ref, o_ref, qb_ref, acc_ref, carry_ref = refs[2 * pages_per_step:]
    p = pl.program_id(1)
    qr = SAMPLE_QR
    u2 = u2_ref[...]

    def head_slab(ref, h):
        return ref[h * HEAD_DIM:(h + 1) * HEAD_DIM, :].astype(BF16)

    def scores(k_ref, mask):
        z = jnp.concatenate(
            [jnp.dot(qb_ref[h * qr:(h + 1) * qr, :], head_slab(k_ref, h), preferred_element_type=F32)
             for h in range(N_HEADS)], axis=0) * (HEAD_DIM ** -0.5 * LOG2E) + bias_ref[...]
        return _sb_scores(z, u2, mask)

    def sweep(k_refs, v_refs, mask):
        blocks = [scores(k_ref, mask) for k_ref in k_refs]
        carry = carry_ref[...]
        acc = [acc_ref[h * qr:(h + 1) * qr, :] for h in range(N_HEADS)]
        for (w, tot), v_ref in zip(blocks, v_refs):
            scale = jnp.exp2(-carry)
            carry = carry + tot
            for h in range(N_HEADS):
                rows = slice(h * qr, (h + 1) * qr)
                acc[h] = acc[h] + scale[rows] * lax.dot_general(
                    w[rows], head_slab(v_ref, h), (((1,), (1,)), ((), ())), preferred_element_type=F32)
        carry_ref[...] = carry
        for h in range(N_HEADS):
            acc_ref[h * qr:(h + 1) * qr, :] = acc[h]

    @pl.when(p == 0)
    def _():
        qb_ref[...] = q_ref[...].astype(BF16)
        carry_ref[...] = jnp.zeros_like(carry_ref)
        acc_ref[...] = jnp.zeros_like(acc_ref)
        qpos = lax.broadcasted_iota(jnp.int32, (N_HEADS * qr, page), 0) % qr
        kpos = lax.broadcasted_iota(jnp.int32, (N_HEADS * qr, page), 1)
        sweep([kn_ref], [vn_ref], kpos < qpos)

    sweep(kc_refs, vc_refs, None)

    @pl.when(p == pl.num_programs(1) - 1)
    def _():
        o_ref[...] = acc_ref[...]


def _attn_sample(q, k, v, cache_k, cache_v, page_table, sb_bias):
    dec_b, dec_t = q.shape[:2]
    n_pool, page = cache_k.shape[:2]
    n_pages = page_table.shape[1]
    pps = SAMPLE_PAGES_PER_STEP
    qr = SAMPLE_QR
    assert page == LANES and dec_t <= qr and n_pages % pps == 0
    rows = N_HEADS * qr
    q_rows = jnp.pad(q.transpose(0, 2, 1, 3), ((0, 0), (0, 0), (0, qr - dec_t), (0, 0))).reshape(dec_b, rows, HEAD_DIM)
    pad = ((0, 0), (0, 0), (0, 0), (0, page - dec_t))
    k_new = jnp.pad(k.transpose(0, 2, 3, 1), pad).reshape(dec_b, D_ATTN, page)
    v_new = jnp.pad(v.transpose(0, 2, 3, 1), pad).reshape(dec_b, D_ATTN, page)
    bias = jnp.broadcast_to(jnp.repeat(sb_bias.astype(F32) * LOG2E, qr)[:, None], (rows, page))
    pool_t = lambda c: c.transpose(0, 2, 3, 1).reshape(n_pool, D_ATTN, page)
    page_block = (None, D_ATTN, page)

    def page_spec(s):
        return pl.BlockSpec(page_block, lambda b, p, pt: (pt[b * n_pages + n_pages - 1 - (p * pps + s)], 0, 0))

    per_seq = lambda b, p, pt: (b, 0, 0)
    out = pl.pallas_call(
        functools.partial(_attn_sample_kernel, page=page, pages_per_step=pps),
        grid_spec=pltpu.PrefetchScalarGridSpec(
            num_scalar_prefetch=1,
            grid=(dec_b, n_pages // pps),
            in_specs=([pl.BlockSpec((None, rows, HEAD_DIM), per_seq),
                       pl.BlockSpec(page_block, per_seq),
                       pl.BlockSpec(page_block, per_seq)]
                      + [page_spec(s) for s in range(pps)] * 2
                      + [pl.BlockSpec((rows, page), lambda b, p, pt: (0, 0)),
                         pl.BlockSpec((2 * page, page), lambda b, p, pt: (0, 0))]),
            out_specs=pl.BlockSpec((None, rows, HEAD_DIM), per_seq),
            scratch_shapes=[pltpu.VMEM((rows, HEAD_DIM), BF16), pltpu.VMEM((rows, HEAD_DIM), F32),
                            pltpu.VMEM((rows, 1), F32)]),
        out_shape=jax.ShapeDtypeStruct((dec_b, rows, HEAD_DIM), F32),
        compiler_params=_cparams(("parallel", "arbitrary")),
        name="attn_sample",
    )(page_table.reshape(-1), q_rows, k_new, v_new,
      *([pool_t(cache_k)] * pps), *([pool_t(cache_v)] * pps), bias, _suffix_matrix(page))
    return out.reshape(dec_b, N_HEADS, qr, HEAD_DIM)[:, :, :dec_t].transpose(0, 2, 1, 3)


def _top4(logits):
    lane = lax.broadcasted_iota(jnp.int32, logits.shape, 1)
    vals, idxs = [], []
    l = logits
    for _ in range(TOP_K):
        m = jnp.max(l, axis=-1, keepdims=True)
        idx = jnp.min(jnp.where(l == m, lane, LANES), axis=-1, keepdims=True)
        vals.append(m)
        idxs.append(idx)
        l = jnp.where(lane == idx, NEG_BIG, l)
    es = [jnp.exp(v - vals[0]) for v in vals]
    inv = 1.0 / (es[0] + es[1] + es[2] + es[3])
    out_i = jnp.zeros(logits.shape, jnp.int32)
    out_g = jnp.zeros(logits.shape, F32)
    for k in range(TOP_K):
        out_i = jnp.where(lane == k, idxs[k], out_i)
        out_g = jnp.where(lane == k, es[k] * inv, out_g)
    return out_i, out_g


def _merge_kernel(*refs, tm, seq_len, tail):
    if seq_len >= tm:
        cb_ref, cc_ref, ch_ref, pcc_ref, pch_ref = refs[:5]
    else:
        cb_ref, cc_ref, ch_ref, h1_ref, h2_ref = refs[:5]
    (ya_ref, ga_ref, gb_ref, x_ref, cw_ref, wc_ref, wa_ref, wo_ref, gn_ref, wr_ref, rb_ref,
     h_ref, hn_ref, ti_ref, tg_ref, ut_ref) = refs[5:]
    i = pl.program_id(0)

    u = cc_ref[...] * ch_ref[...]
    row = lax.broadcasted_iota(jnp.int32, u.shape, 0)
    if seq_len >= tm:
        at_start = (i % (seq_len // tm)) == 0
        up = jnp.where(at_start, 0.0, pcc_ref[...] * pch_ref[...])
        u1 = jnp.where(row == 0, up[SUBLANES - 1:SUBLANES], pltpu.roll(u, 1, axis=0))
        u2 = jnp.where(row == 0, up[SUBLANES - 2:SUBLANES - 1],
                       jnp.where(row == 1, up[SUBLANES - 1:SUBLANES], pltpu.roll(u, 2, axis=0)))
    else:
        u1 = jnp.where(row % seq_len == 0, h1_ref[...], pltpu.roll(u, 1, axis=0))
        u2 = jnp.where(row % seq_len < 2, h2_ref[...], pltpu.roll(u, 2, axis=0))
    ut_ref[...] = u[tm - tail:]
    cw = cw_ref[...]
    y_conv = cb_ref[...] * (cw[0:1] * u2 + cw[1:2] * u1 + cw[2:3] * u)

    merged = (jax.nn.sigmoid(ga_ref[...]) * _dot(y_conv, wc_ref[...])
              + jax.nn.sigmoid(gb_ref[...]) * _dot(ya_ref[...], wa_ref[...]))
    h = x_ref[...] + _dot(merged, wo_ref[...])
    h_ref[...] = h
    inv = lax.rsqrt(jnp.mean(h * h, axis=-1, keepdims=True) + RMS_EPS)
    hn = h * inv * gn_ref[...]
    hn_ref[...] = hn
    logits = _dot(hn, wr_ref[...]) + rb_ref[...]
    ti, tg = _top4(logits)
    ti_ref[...] = ti
    tg_ref[...] = tg


def _merge(conv3, prev_or_hist, ya, gates, x, conv_w, wc, wa, wo, gn, wr, rb, *, tm, seq_len):
    t = x.shape[0]
    n_i = t // tm
    tail = min(tm, SUBLANES) if seq_len >= tm else tm
    const = lambda i: (0, 0)
    once = pl.Buffered(1)
    if seq_len >= tm:
        r = tm // SUBLANES
        prev_map_c = lambda i: (jnp.maximum(i * r - 1, 0), 1)
        prev_map_h = lambda i: (jnp.maximum(i * r - 1, 0), 2)
        hist_specs = [pl.BlockSpec((SUBLANES, D_CONV), prev_map_c), pl.BlockSpec((SUBLANES, D_CONV), prev_map_h)]
        hist_args = (conv3, conv3)
    else:
        hist_specs = [pl.BlockSpec((tm, D_CONV), lambda i: (i, 0))] * 2
        hist_args = prev_or_hist
    in_specs = ([pl.BlockSpec((tm, D_CONV), lambda i: (i, 0)),
                 pl.BlockSpec((tm, D_CONV), lambda i: (i, 1)),
                 pl.BlockSpec((tm, D_CONV), lambda i: (i, 2))]
                + hist_specs
                + [pl.BlockSpec((tm, D_ATTN), lambda i: (i, 0)),
                   pl.BlockSpec((tm, D_MODEL), lambda i: (i, 0)),
                   pl.BlockSpec((tm, D_MODEL), lambda i: (i, 1)),
                   pl.BlockSpec((tm, D_MODEL), lambda i: (i, 0)),
                   pl.BlockSpec((SUBLANES, D_CONV), const),
                   pl.BlockSpec((D_CONV, D_MODEL), const, pipeline_mode=once),
                   pl.BlockSpec((D_ATTN, D_MODEL), const, pipeline_mode=once),
                   pl.BlockSpec((D_MODEL, D_MODEL), const, pipeline_mode=once),
                   pl.BlockSpec((1, D_MODEL), const),
                   pl.BlockSpec((D_MODEL, LANES), const),
                   pl.BlockSpec((1, LANES), const)])
    return pl.pallas_call(
        functools.partial(_merge_kernel, tm=tm, seq_len=seq_len, tail=tail),
        grid=(n_i,),
        in_specs=in_specs,
        out_specs=[pl.BlockSpec((tm, D_MODEL), lambda i: (i, 0)),
                   pl.BlockSpec((tm, D_MODEL), lambda i: (i, 0)),
                   pl.BlockSpec((tm, LANES), lambda i: (i, 0)),
                   pl.BlockSpec((tm, LANES), lambda i: (i, 0)),
                   pl.BlockSpec((tail, D_CONV), lambda i: (i, 0))],
        out_shape=[jax.ShapeDtypeStruct((t, D_MODEL), F32),
                   jax.ShapeDtypeStruct((t, D_MODEL), F32),
                   jax.ShapeDtypeStruct((t, LANES), jnp.int32),
                   jax.ShapeDtypeStruct((t, LANES), F32),
                   jax.ShapeDtypeStruct((n_i * tail, D_CONV), F32)],
        compiler_params=_cparams(("parallel",)),
        name="merge",
    )(conv3, conv3, conv3, *hist_args, ya, gates, gates, x, conv_w, wc, wa, wo, gn, wr, rb)


MOE_TM = 256
MOE_TF = 1024
MOE_TN = 1024


def _prefetched_weights(sched, j, i, n_passes, copies, wbuf_ref, wb_ref):
    be_ref, first_ref, nxt_ref, gidx_ref, s_ref = sched
    n_groups = s_ref[1]

    @pl.when(first_ref[i] == 1)
    def _():
        @pl.when((j == 0) & (gidx_ref[i] == 0))
        def _():
            for c in copies(be_ref[i], j):
                c.start()

        for c in copies(be_ref[i], j):
            c.wait()
        wb_ref[...] = wbuf_ref[...].astype(BF16)
        j_next = jnp.where(gidx_ref[i] == n_groups - 1, j + 1, j)

        @pl.when(j_next < n_passes)
        def _():
            for c in copies(nxt_ref[i], j_next):
                c.start()


def _moe_up_kernel(*refs):
    sched, (x_ref, w_ref, bg_ref, bl_ref, o_ref, wbuf_ref, wb_ref, sem_ref) = refs[:5], refs[5:]
    j, i = pl.program_id(0), pl.program_id(1)

    def copies(expert, jc):
        return [pltpu.make_async_copy(
            w_ref.at[expert, :, pl.ds(pl.multiple_of(half * D_FF + jc * MOE_TF, MOE_TF), MOE_TF)],
            wbuf_ref.at[half], sem_ref.at[0]) for half in range(2)]

    _prefetched_weights(sched, j, i, pl.num_programs(0), copies, wbuf_ref, wb_ref)

    @pl.when(i < sched[4][0])
    def _():
        x = x_ref[...].astype(BF16)
        h_glu = jnp.minimum(_dot(x, wb_ref[0]) + bg_ref[0], SWIGLU_LIMIT)
        h_lin = jnp.clip(_dot(x, wb_ref[1]) + bl_ref[0], -SWIGLU_LIMIT, SWIGLU_LIMIT)
        o_ref[...] = (h_glu * jax.nn.sigmoid(SWIGLU_ALPHA * h_glu) * (h_lin + 1.0)).astype(BF16)

    @pl.when(i >= sched[4][0])
    def _():
        o_ref[...] = jnp.zeros_like(o_ref)


def _moe_down_kernel(*refs):
    sched, (a_ref, w_ref, b_ref, o_ref, wbuf_ref, wb_ref, sem_ref) = refs[:5], refs[5:]
    j, i = pl.program_id(0), pl.program_id(1)

    def copies(expert, jc):
        return [pltpu.make_async_copy(w_ref.at[expert, :, pl.ds(pl.multiple_of(jc * MOE_TN, MOE_TN), MOE_TN)],
                                      wbuf_ref, sem_ref.at[0])]

    _prefetched_weights(sched, j, i, pl.num_programs(0), copies, wbuf_ref, wb_ref)

    @pl.when(i < sched[4][0])
    def _():
        o_ref[...] = _dot(a_ref[...], wb_ref[...]) + b_ref[0]

    @pl.when(i >= sched[4][0])
    def _():
        o_ref[...] = jnp.zeros_like(o_ref)


def _moe_experts(xs, sched, w1, b1, w2, b2):
    n_rows = xs.shape[0]
    n_i = n_rows // MOE_TM
    n_f = D_FF // MOE_TF
    b1r = b1.reshape(N_EXPERTS, 1, 2 * D_FF)
    b2r = b2.reshape(N_EXPERTS, 1, D_MODEL)
    hbm = pl.BlockSpec(memory_space=pl.ANY)
    act = pl.pallas_call(
        _moe_up_kernel,
        grid_spec=pltpu.PrefetchScalarGridSpec(
            num_scalar_prefetch=5,
            grid=(n_f, n_i),
            in_specs=[pl.BlockSpec((MOE_TM, D_MODEL), lambda j, i, be, *_: (i, 0)),
                      hbm,
                      pl.BlockSpec((1, 1, MOE_TF), lambda j, i, be, *_: (be[i], 0, j)),
                      pl.BlockSpec((1, 1, MOE_TF), lambda j, i, be, *_: (be[i], 0, n_f + j))],
            out_specs=pl.BlockSpec((MOE_TM, MOE_TF), lambda j, i, be, *_: (i, j)),
            scratch_shapes=[pltpu.VMEM((2, D_MODEL, MOE_TF), F32), pltpu.VMEM((2, D_MODEL, MOE_TF), BF16),
                            pltpu.SemaphoreType.DMA((1,))]),
        out_shape=jax.ShapeDtypeStruct((n_rows, D_FF), BF16),
        compiler_params=_cparams(("arbitrary", "arbitrary")),
        name="moe_up",
    )(*sched, xs, w1, b1r, b1r)
    return pl.pallas_call(
        _moe_down_kernel,
        grid_spec=pltpu.PrefetchScalarGridSpec(
            num_scalar_prefetch=5,
            grid=(D_MODEL // MOE_TN, n_i),
            in_specs=[pl.BlockSpec((MOE_TM, D_FF), lambda j, i, be, *_: (i, 0)),
                      hbm,
                      pl.BlockSpec((1, 1, MOE_TN), lambda j, i, be, *_: (be[i], 0, j))],
            out_specs=pl.BlockSpec((MOE_TM, MOE_TN), lambda j, i, be, *_: (i, j)),
            scratch_shapes=[pltpu.VMEM((D_FF, MOE_TN), F32), pltpu.VMEM((D_FF, MOE_TN), BF16),
                            pltpu.SemaphoreType.DMA((1,))]),
        out_shape=jax.ShapeDtypeStruct((n_rows, D_MODEL), F32),
        compiler_params=_cparams(("arbitrary", "arbitrary")),
        name="moe_down",
    )(*sched, act, w2, b2r)


GATHER_UNROLL = 8


def _gather_rows_kernel(tok_ref, src_ref, o_ref, sem_ref):
    i = pl.program_id(0)

    def for_rows(step, slot, fn):
        def body(r, c):
            row = step * MOE_TM + r
            fn(pltpu.make_async_copy(src_ref.at[pl.ds(tok_ref[row], 1)], o_ref.at[pl.ds(row, 1)], sem_ref.at[slot]))
            return c
        lax.fori_loop(0, MOE_TM, body, 0, unroll=GATHER_UNROLL)

    slot = i % 2
    for_rows(i, slot, lambda c: c.start())

    @pl.when(i > 0)
    def _():
        for_rows(i - 1, 1 - slot, lambda c: c.wait())

    @pl.when(i == pl.num_programs(0) - 1)
    def _():
        for_rows(i, slot, lambda c: c.wait())


def _gather_rows(src, tok):
    n_rows = tok.shape[0]
    hbm = pl.BlockSpec(memory_space=pl.ANY)
    return pl.pallas_call(
        _gather_rows_kernel,
        grid_spec=pltpu.PrefetchScalarGridSpec(
            num_scalar_prefetch=1, grid=(n_rows // MOE_TM,), in_specs=[hbm], out_specs=hbm,
            scratch_shapes=[pltpu.SemaphoreType.DMA((2,))]),
        out_shape=jax.ShapeDtypeStruct((n_rows, src.shape[1]), src.dtype),
        compiler_params=_cparams(("arbitrary",)),
        name="gather_rows",
    )(tok, src)


def _route(top_i):
    t = top_i.shape[0]
    n_assign = t * TOP_K
    flat_e = top_i.reshape(-1)
    experts = jnp.arange(N_EXPERTS, dtype=jnp.int32)
    order = jnp.argsort(flat_e).astype(jnp.int32)
    counts = jnp.sum((flat_e[:, None] == experts[None, :]).astype(jnp.int32), axis=0)
    offsets = jnp.cumsum(counts) - counts
    padded = (counts + MOE_TM - 1) // MOE_TM * MOE_TM
    padded_end = jnp.cumsum(padded)
    padded_off = padded_end - padded
    n_tiles = -(-n_assign // MOE_TM) + N_EXPERTS
    tile_start = jnp.arange(n_tiles, dtype=jnp.int32) * MOE_TM
    block_expert = jnp.minimum(jnp.sum((tile_start[:, None] >= padded_end[None, :]).astype(jnp.int32), axis=1),
                               N_EXPERTS - 1)
    in_group = (tile_start - padded_off[block_expert])[:, None] + jnp.arange(MOE_TM, dtype=jnp.int32)[None, :]
    valid = in_group < counts[block_expert][:, None]
    sorted_idx = jnp.clip(offsets[block_expert][:, None] + in_group, 0, n_assign - 1)
    row_tok = jnp.where(valid, order[sorted_idx] // TOP_K, t).reshape(-1)
    sorted_e = flat_e[order]
    dest = padded_off[sorted_e] + (jnp.arange(n_assign, dtype=jnp.int32) - offsets[sorted_e])
    _, pos = lax.sort((order, dest), num_keys=1)
    tiles = jnp.arange(n_tiles, dtype=jnp.int32)
    first = jnp.concatenate([jnp.ones((1,), jnp.bool_), block_expert[1:] != block_expert[:-1]])
    group = jnp.cumsum(first.astype(jnp.int32)) - 1
    next_first = lax.cummin(jnp.where(first, tiles, n_tiles), reverse=True)
    next_first = jnp.concatenate([next_first[1:], jnp.full((1,), n_tiles, jnp.int32)])
    next_expert = jnp.where(next_first < n_tiles, block_expert[jnp.minimum(next_first, n_tiles - 1)],
                            block_expert[0])
    scalars = jnp.stack([padded_end[-1] // MOE_TM, group[-1] + 1]).astype(jnp.int32)
    sched = (block_expert, first.astype(jnp.int32), next_expert, group, scalars)
    return row_tok, pos.reshape(t, TOP_K), sched


COMBINE_UNROLL = 8


def _combine_kernel(pos_ref, h_ref, g_ref, gn_ref, ys_ref, o_ref, buf_ref, sem_ref, *, tm):
    i = pl.program_id(0)
    n_i = pl.num_programs(0)

    def row_copy(tile, slot, r, k):
        src = pos_ref[(tile * tm + r) * TOP_K + k]
        return pltpu.make_async_copy(ys_ref.at[pl.ds(src, 1)], buf_ref.at[slot, k, pl.ds(r, 1)], sem_ref.at[slot])

    def for_rows(tile, slot, fn):
        def body(r, c):
            for k in range(TOP_K):
                fn(row_copy(tile, slot, r, k))
            return c
        lax.fori_loop(0, tm, body, 0, unroll=COMBINE_UNROLL)

    slot = i % 2

    @pl.when(i == 0)
    def _():
        for_rows(0, 0, lambda c: c.start())

    @pl.when(i + 1 < n_i)
    def _():
        for_rows(i + 1, 1 - slot, lambda c: c.start())

    for_rows(i, slot, lambda c: c.wait())
    y = h_ref[...]
    g = g_ref[...]
    for k in range(TOP_K):
        y = y + g[:, k:k + 1] * buf_ref[slot, k]
    inv = lax.rsqrt(jnp.mean(y * y, axis=-1, keepdims=True) + RMS_EPS)
    o_ref[...] = y * inv * gn_ref[...]


def _combine(h, ys, pos, g, gn, *, tm):
    t = h.shape[0]
    return pl.pallas_call(
        functools.partial(_combine_kernel, tm=tm),
        grid_spec=pltpu.PrefetchScalarGridSpec(
            num_scalar_prefetch=1,
            grid=(t // tm,),
            in_specs=[pl.BlockSpec((tm, D_MODEL), lambda i, pos: (i, 0)),
                      pl.BlockSpec((tm, LANES), lambda i, pos: (i, 0)),
                      pl.BlockSpec((1, D_MODEL), lambda i, pos: (0, 0)),
                      pl.BlockSpec(memory_space=pl.ANY)],
            out_specs=pl.BlockSpec((tm, D_MODEL), lambda i, pos: (i, 0)),
            scratch_shapes=[pltpu.VMEM((2, TOP_K, tm, D_MODEL), F32), pltpu.SemaphoreType.DMA((2,))]),
        out_shape=jax.ShapeDtypeStruct((t, D_MODEL), F32),
        compiler_params=_cparams(("arbitrary",)),
        name="combine",
    )(pos.reshape(-1), h, g, gn, ys)


def kernel(x_prompt, x_sample, cache_k, cache_v, state_conv, page_table, norm_mix, w_in, conv_w, sb_bias,
           w_br_conv, w_br_attn, w_o, norm_ffn, router_w, router_b, w_mlp1, b_mlp1, w_mlp2, b_mlp2, norm_final):
    assert w_in.shape[0] == 1, "single layer"
    batch, seq, d = x_prompt.shape
    dec_b, dec_t, _ = x_sample.shape
    n_pool, page = cache_k.shape[1], cache_k.shape[2]
    t_p, t_s = batch * seq, dec_b * dec_t
    assert dec_t > CONV_W - 1

    g_mix = norm_mix[0].reshape(1, d)
    g_ffn = norm_ffn[0].reshape(1, d)
    g_fin = norm_final.reshape(1, d)
    cw = jnp.pad(conv_w[0], ((0, SUBLANES - CONV_W), (0, 0)))
    wr = jnp.pad(router_w[0], ((0, 0), (0, LANES - N_EXPERTS)))
    rb = jnp.pad(router_b[0].astype(F32), (0, LANES - N_EXPERTS), constant_values=NEG_BIG).reshape(1, LANES)

    w_in_b = w_in[0].astype(BF16)
    weights = (w_br_conv[0].astype(BF16), w_br_attn[0].astype(BF16), w_o[0].astype(BF16), g_ffn,
               wr.astype(BF16), rb)

    xp = x_prompt.reshape(t_p, d)
    conv3, q, k, v, gates = _inproj(xp, g_mix, w_in_b, tm=1024)
    ya = _attn_prompt(q, k, v, sb_bias[0], batch, seq)
    h_p, hn_p, ti_p, tg_p, ut_p = _merge(conv3, None, ya, gates, xp, cw, *weights, tm=256, seq_len=seq)

    xs = x_sample.reshape(t_s, d)
    conv3_s, q_s, k_s, v_s, gates_s = _inproj(xs, g_mix, w_in_b, tm=t_s)
    new_rows = (dec_b, dec_t, N_HEADS, HEAD_DIM)
    ya_s = _attn_sample(q_s.reshape(new_rows), k_s.reshape(new_rows), v_s.reshape(new_rows),
                        cache_k[0], cache_v[0], page_table, sb_bias[0])
    hist = state_conv[0].astype(F32)
    zero = jnp.zeros((dec_b, dec_t - 2, D_CONV), F32)
    hist1 = jnp.concatenate([hist[:, 1:2], jnp.zeros((dec_b, dec_t - 1, D_CONV), F32)], axis=1)
    hist2 = jnp.concatenate([hist, zero], axis=1)
    h_s, hn_s, ti_s, tg_s, u_s = _merge(
        conv3_s, (hist1.reshape(t_s, D_CONV), hist2.reshape(t_s, D_CONV)), ya_s.reshape(t_s, D_ATTN),
        gates_s, xs, cw, *weights, tm=t_s, seq_len=dec_t)

    t_all = t_p + t_s
    top_i = jnp.concatenate([ti_p[:, :TOP_K], ti_s[:, :TOP_K]], axis=0)
    row_tok, pos, sched = _route(top_i)
    hn_all = jnp.concatenate([hn_p, hn_s, jnp.zeros((1, d), F32)], axis=0)
    ys = _moe_experts(_gather_rows(hn_all, row_tok), sched, w_mlp1[0], b_mlp1[0], w_mlp2[0], b_mlp2[0])
    y_p = _combine(h_p, ys, pos[:t_p], tg_p, g_fin, tm=128)
    y_s = _combine(h_s, ys, pos[t_p:], tg_s, g_fin, tm=t_s)

    n_hist = CONV_W - 1
    conv_p = ut_p.reshape(batch, seq // 256, SUBLANES, D_CONV)[:, -1, SUBLANES - n_hist:]
    conv_s = u_s.reshape(dec_b, dec_t, D_CONV)[:, dec_t - n_hist:]
    return (y_p.reshape(batch, seq, d), y_s.reshape(dec_b, dec_t, d),
            k.reshape(1, batch, seq, N_HEADS, HEAD_DIM), v.reshape(1, batch, seq, N_HEADS, HEAD_DIM),
            conv_p[None], k_s.reshape(1, dec_b, dec_t, N_HEADS, HEAD_DIM),
            v_s.reshape(1, dec_b, dec_t, N_HEADS, HEAD_DIM), conv_s[None])
```

```python
import functools

import jax
import jax.numpy as jnp
from jax import lax
from jax.experimental import pallas as pl
from jax.experimental.pallas import tpu as pltpu

F32 = jnp.float32
BF16 = jnp.bfloat16

D_MODEL = 2048
N_HEADS = 16
HEAD_DIM = 64
D_ATTN = N_HEADS * HEAD_DIM
D_CONV = D_MODEL // 2
CONV_W = 3
N_EXPERTS = 32
TOP_K = 4
D_FF = D_MODEL
SWIGLU_LIMIT = 7.0
SWIGLU_ALPHA = 1.702
RMS_EPS = 1e-5
D_IN = 3 * D_CONV + 3 * D_ATTN + 2 * D_MODEL

LANES = 128
SUBLANES = 8
VMEM_LIMIT = 56 * 1024 * 1024
NEG_BIG = -3.0e38


def _cparams(sem):
    return pltpu.CompilerParams(dimension_semantics=sem, vmem_limit_bytes=VMEM_LIMIT)


def _dot(a, b):
    return jnp.dot(a.astype(BF16), b.astype(BF16), preferred_element_type=F32)


IN_TN = 512
_IN_GROUPS = ((0, 3 * D_CONV // IN_TN),
              (3 * D_CONV // IN_TN, D_ATTN // IN_TN),
              ((3 * D_CONV + D_ATTN) // IN_TN, D_ATTN // IN_TN),
              ((3 * D_CONV + 2 * D_ATTN) // IN_TN, D_ATTN // IN_TN),
              ((3 * D_CONV + 3 * D_ATTN) // IN_TN, 2 * D_MODEL // IN_TN))


def _inproj_kernel(x_ref, g_ref, w_ref, *refs):
    out_refs, xn_ref = refs[:-1], refs[-1]
    j = pl.program_id(1)

    @pl.when(j == 0)
    def _():
        x = x_ref[...]
        inv = lax.rsqrt(jnp.mean(x * x, axis=-1, keepdims=True) + RMS_EPS)
        xn_ref[...] = (x * inv * g_ref[...]).astype(xn_ref.dtype)

    acc = _dot(xn_ref[...], w_ref[...])
    for o_ref, (a, n) in zip(out_refs, _IN_GROUPS):
        @pl.when((j >= a) & (j < a + n))
        def _(o_ref=o_ref):
            o_ref[...] = acc


def _inproj(x, g, w, *, tm):
    t = x.shape[0]
    n_j = D_IN // IN_TN

    def omap(a, n):
        return lambda i, j: (i, jnp.clip(j - a, 0, n - 1))

    return pl.pallas_call(
        _inproj_kernel,
        grid=(t // tm, n_j),
        in_specs=[pl.BlockSpec((tm, D_MODEL), lambda i, j: (i, 0)),
                  pl.BlockSpec((1, D_MODEL), lambda i, j: (0, 0)),
                  pl.BlockSpec((D_MODEL, IN_TN), lambda i, j: (0, j))],
        out_specs=[pl.BlockSpec((tm, IN_TN), omap(a, n)) for a, n in _IN_GROUPS],
        out_shape=[jax.ShapeDtypeStruct((t, n * IN_TN), F32) for _, n in _IN_GROUPS],
        scratch_shapes=[pltpu.VMEM((tm, D_MODEL), BF16)],
        compiler_params=_cparams(("parallel", "arbitrary")),
        name="inproj",
    )(x, g, w)


LOG2E = 1.4426950408889634
SP_LINEAR_ABOVE = 64.0


def _suffix_matrix(tk):
    m = lax.broadcasted_iota(jnp.int32, (2 * tk, tk), 0) % tk
    j = lax.broadcasted_iota(jnp.int32, (2 * tk, tk), 1)
    return (m > j).astype(BF16)


def _sb_scores(z, u2, mask):
    sp = jnp.where(z > SP_LINEAR_ABOVE, z, jnp.log(1.0 + jnp.exp2(z)) * LOG2E)
    log_beta = z - sp
    if mask is not None:
        sp = jnp.where(mask, sp, 0.0)
    hi = pltpu.bitcast(pltpu.bitcast(sp, jnp.int32) & jnp.int32(-65536), F32)
    lo = sp - hi
    hl = jnp.concatenate([hi.astype(BF16), lo.astype(BF16)], axis=1)
    half = hl.shape[0] // 2
    s = jnp.concatenate([jnp.dot(hl[:half], u2, preferred_element_type=F32),
                         jnp.dot(hl[half:], u2, preferred_element_type=F32)], axis=0)
    w = jnp.exp2(log_beta - s)
    if mask is not None:
        w = jnp.where(mask, w, 0.0)
    return w.astype(BF16), s[:, 0:1] + sp[:, 0:1]


ATT_T = 256


def _attn_prompt_kernel(q_ref, k_ref, v_ref, bias_ref, u2_ref, o_ref, kb_ref, vb_ref, acc_ref):
    qi = pl.program_id(2)
    t = ATT_T

    @pl.when(qi == 0)
    def _():
        kb_ref[...] = k_ref[...].astype(BF16)
        vb_ref[...] = v_ref[...].astype(BF16)

    q = q_ref[...] * (HEAD_DIM ** -0.5 * LOG2E)
    lane = lax.broadcasted_iota(jnp.int32, (t, LANES), 1)
    q2 = jnp.concatenate([jnp.where(lane < HEAD_DIM, q, 0.0), jnp.where(lane >= HEAD_DIM, q, 0.0)],
                         axis=0).astype(BF16)
    bias = bias_ref[0]
    u2 = u2_ref[...]

    def logits(j):
        kb = kb_ref[pl.ds(pl.multiple_of(j * t, t), t), :]
        return lax.dot_general(q2, kb, (((1,), (1,)), ((), ())), preferred_element_type=F32) + bias

    def accumulate(j, w, carry):
        vb = vb_ref[pl.ds(pl.multiple_of(j * t, t), t), :]
        acc_ref[...] += jnp.exp2(-carry) * jnp.dot(w, vb, preferred_element_type=F32)

    acc_ref[...] = jnp.zeros_like(acc_ref)
    row = lax.broadcasted_iota(jnp.int32, (2 * t, t), 0) % t
    col = lax.broadcasted_iota(jnp.int32, (2 * t, t), 1)
    w, tot = _sb_scores(logits(qi), u2, col < row)
    z = logits(jnp.maximum(qi - 1, 0))

    def body(n, c):
        z, w, tot, carry = c
        z_next = logits(jnp.maximum(qi - n - 2, 0))
        nxt = _sb_scores(z, u2, None)
        accumulate(qi - n, w, carry)
        return (z_next,) + nxt + (carry + tot,)

    _, w, _, carry = lax.fori_loop(0, qi, body, (z, w, tot, jnp.zeros((2 * t, 1), F32)))
    accumulate(0, w, carry)
    acc = acc_ref[...]
    o_ref[...] = jnp.where(lane < HEAD_DIM, acc[:t], acc[t:])


def _attn_prompt(q, k, v, sb_bias, batch, seq):
    t = ATT_T
    n_q = seq // t
    n_hp = N_HEADS // 2
    bias = jnp.broadcast_to(jnp.repeat(sb_bias.astype(F32).reshape(n_hp, 2) * LOG2E, t, axis=1)[:, :, None],
                            (n_hp, 2 * t, t))
    return pl.pallas_call(
        _attn_prompt_kernel,
        grid=(batch, n_hp, n_q),
        in_specs=[pl.BlockSpec((t, LANES), lambda b, h, i: (b * n_q + i, h)),
                  pl.BlockSpec((seq, LANES), lambda b, h, i: (b, h)),
                  pl.BlockSpec((seq, LANES), lambda b, h, i: (b, h)),
                  pl.BlockSpec((1, 2 * t, t), lambda b, h, i: (h, 0, 0)),
                  pl.BlockSpec((2 * t, t), lambda b, h, i: (0, 0))],
        out_specs=pl.BlockSpec((t, LANES), lambda b, h, i: (b * n_q + i, h)),
        out_shape=jax.ShapeDtypeStruct((batch * seq, D_ATTN), F32),
        scratch_shapes=[pltpu.VMEM((seq, LANES), BF16), pltpu.VMEM((seq, LANES), BF16),
                        pltpu.VMEM((2 * t, LANES), F32)],
        compiler_params=_cparams(("parallel", "parallel", "arbitrary")),
        name="attn_prompt",
    )(q, k, v, bias, _suffix_matrix(t))


SAMPLE_QR = 16
SAMPLE_PAGES_PER_STEP = 8


def _attn_sample_kernel(pt_ref, q_ref, kn_ref, vn_ref, *refs, page, pages_per_step):
    kc_refs, vc_refs = refs[:pages_per_step], refs[pages_per_step:2 * pages_per_step]
    bias_ref, u2_ref, o_ref, qb_ref, acc_ref, carry_ref = refs[2 * pages_per_step:]
    p = pl.program_id(1)
    qr = SAMPLE_QR
    u2 = u2_ref[...]

    def head_slab(ref, h):
        return ref[h * HEAD_DIM:(h + 1) * HEAD_DIM, :].astype(BF16)

    def scores(k_ref, mask):
        z = jnp.concatenate(
            [jnp.dot(qb_ref[h * qr:(h + 1) * qr, :], head_slab(k_ref, h), preferred_element_type=F32)
             for h in range(N_HEADS)], axis=0) * (HEAD_DIM ** -0.5 * LOG2E) + bias_ref[...]
        return _sb_scores(z, u2, mask)

    def sweep(k_refs, v_refs, mask):
        blocks = [scores(k_ref, mask) for k_ref in k_refs]
        carry = carry_ref[...]
        acc = [acc_ref[h * qr:(h + 1) * qr, :] for h in range(N_HEADS)]
        for (w, tot), v_ref in zip(blocks, v_refs):
            scale = jnp.exp2(-carry)
            carry = carry + tot
            for h in range(N_HEADS):
                rows = slice(h * qr, (h + 1) * qr)
                acc[h] = acc[h] + scale[rows] * lax.dot_general(
                    w[rows], head_slab(v_ref, h), (((1,), (1,)), ((), ())), preferred_element_type=F32)
        carry_ref[...] = carry
        for h in range(N_HEADS):
            acc_ref[h * qr:(h + 1) * qr, :] = acc[h]

    @pl.when(p == 0)
    def _():
        qb_ref[...] = q_ref[...].astype(BF16)
        carry_ref[...] = jnp.zeros_like(carry_ref)
        acc_ref[...] = jnp.zeros_like(acc_ref)
        qpos = lax.broadcasted_iota(jnp.int32, (N_HEADS * qr, page), 0) % qr
        kpos = lax.broadcasted_iota(jnp.int32, (N_HEADS * qr, page), 1)
        sweep([kn_ref], [vn_ref], kpos < qpos)

    sweep(kc_refs, vc_refs, None)

    @pl.when(p == pl.num_programs(1) - 1)
    def _():
        o_ref[...] = acc_ref[...]


def _attn_sample(q, k, v, cache_k, cache_v, page_table, sb_bias):
    dec_b, dec_t = q.shape[:2]
    n_pool, page = cache_k.shape[:2]
    n_pages = page_table.shape[1]
    pps = SAMPLE_PAGES_PER_STEP
    qr = SAMPLE_QR
    assert page == LANES and dec_t <= qr and n_pages % pps == 0
    rows = N_HEADS * qr
    q_rows = jnp.pad(q.transpose(0, 2, 1, 3), ((0, 0), (0, 0), (0, qr - dec_t), (0, 0))).reshape(dec_b, rows, HEAD_DIM)
    pad = ((0, 0), (0, 0), (0, 0), (0, page - dec_t))
    k_new = jnp.pad(k.transpose(0, 2, 3, 1), pad).reshape(dec_b, D_ATTN, page)
    v_new = jnp.pad(v.transpose(0, 2, 3, 1), pad).reshape(dec_b, D_ATTN, page)
    bias = jnp.broadcast_to(jnp.repeat(sb_bias.astype(F32) * LOG2E, qr)[:, None], (rows, page))
    pool_t = lambda c: c.transpose(0, 2, 3, 1).reshape(n_pool, D_ATTN, page)
    page_block = (None, D_ATTN, page)

    def page_spec(s):
        return pl.BlockSpec(page_block, lambda b, p, pt: (pt[b * n_pages + n_pages - 1 - (p * pps + s)], 0, 0))

    per_seq = lambda b, p, pt: (b, 0, 0)
    out = pl.pallas_call(
        functools.partial(_attn_sample_kernel, page=page, pages_per_step=pps),
        grid_spec=pltpu.PrefetchScalarGridSpec(
            num_scalar_prefetch=1,
            grid=(dec_b, n_pages // pps),
            in_specs=([pl.BlockSpec((None, rows, HEAD_DIM), per_seq),
                       pl.BlockSpec(page_block, per_seq),
                       pl.BlockSpec(page_block, per_seq)]
                      + [page_spec(s) for s in range(pps)] * 2
                      + [pl.BlockSpec((rows, page), lambda b, p, pt: (0, 0)),
                         pl.BlockSpec((2 * page, page), lambda b, p, pt: (0, 0))]),
            out_specs=pl.BlockSpec((None, rows, HEAD_DIM), per_seq),
            scratch_shapes=[pltpu.VMEM((rows, HEAD_DIM), BF16), pltpu.VMEM((rows, HEAD_DIM), F32),
                            pltpu.VMEM((rows, 1), F32)]),
        out_shape=jax.ShapeDtypeStruct((dec_b, rows, HEAD_DIM), F32),
        compiler_params=_cparams(("parallel", "arbitrary")),
        name="attn_sample",
    )(page_table.reshape(-1), q_rows, k_new, v_new,
      *([pool_t(cache_k)] * pps), *([pool_t(cache_v)] * pps), bias, _suffix_matrix(page))
    return out.reshape(dec_b, N_HEADS, qr, HEAD_DIM)[:, :, :dec_t].transpose(0, 2, 1, 3)


def _top4(logits):
    lane = lax.broadcasted_iota(jnp.int32, logits.shape, 1)
    vals, idxs = [], []
    l = logits
    for _ in range(TOP_K):
        m = jnp.max(l, axis=-1, keepdims=True)
        idx = jnp.min(jnp.where(l == m, lane, LANES), axis=-1, keepdims=True)
        vals.append(m)
        idxs.append(idx)
        l = jnp.where(lane == idx, NEG_BIG, l)
    es = [jnp.exp(v - vals[0]) for v in vals]
    inv = 1.0 / (es[0] + es[1] + es[2] + es[3])
    out_i = jnp.zeros(logits.shape, jnp.int32)
    out_g = jnp.zeros(logits.shape, F32)
    for k in range(TOP_K):
        out_i = jnp.where(lane == k, idxs[k], out_i)
        out_g = jnp.where(lane == k, es[k] * inv, out_g)
    return out_i, out_g


def _merge_kernel(*refs, tm, seq_len, tail):
    if seq_len >= tm:
        cb_ref, cc_ref, ch_ref, pcc_ref, pch_ref = refs[:5]
    else:
        cb_ref, cc_ref, ch_ref, h1_ref, h2_ref = refs[:5]
    (ya_ref, ga_ref, gb_ref, x_ref, cw_ref, wc_ref, wa_ref, wo_ref, gn_ref, wr_ref, rb_ref,
     h_ref, hn_ref, ti_ref, tg_ref, ut_ref) = refs[5:]
    i = pl.program_id(0)

    u = cc_ref[...] * ch_ref[...]
    row = lax.broadcasted_iota(jnp.int32, u.shape, 0)
    if seq_len >= tm:
        at_start = (i % (seq_len // tm)) == 0
        up = jnp.where(at_start, 0.0, pcc_ref[...] * pch_ref[...])
        u1 = jnp.where(row == 0, up[SUBLANES - 1:SUBLANES], pltpu.roll(u, 1, axis=0))
        u2 = jnp.where(row == 0, up[SUBLANES - 2:SUBLANES - 1],
                       jnp.where(row == 1, up[SUBLANES - 1:SUBLANES], pltpu.roll(u, 2, axis=0)))
    else:
        u1 = jnp.where(row % seq_len == 0, h1_ref[...], pltpu.roll(u, 1, axis=0))
        u2 = jnp.where(row % seq_len < 2, h2_ref[...], pltpu.roll(u, 2, axis=0))
    ut_ref[...] = u[tm - tail:]
    cw = cw_ref[...]
    y_conv = cb_ref[...] * (cw[0:1] * u2 + cw[1:2] * u1 + cw[2:3] * u)

    merged = (jax.nn.sigmoid(ga_ref[...]) * _dot(y_conv, wc_ref[...])
              + jax.nn.sigmoid(gb_ref[...]) * _dot(ya_ref[...], wa_ref[...]))
    h = x_ref[...] + _dot(merged, wo_ref[...])
    h_ref[...] = h
    inv = lax.rsqrt(jnp.mean(h * h, axis=-1, keepdims=True) + RMS_EPS)
    hn = h * inv * gn_ref[...]
    hn_ref[...] = hn
    logits = _dot(hn, wr_ref[...]) + rb_ref[...]
    ti, tg = _top4(logits)
    ti_ref[...] = ti
    tg_ref[...] = tg


def _merge(conv3, prev_or_hist, ya, gates, x, conv_w, wc, wa, wo, gn, wr, rb, *, tm, seq_len):
    t = x.shape[0]
    n_i = t // tm
    tail = min(tm, SUBLANES) if seq_len >= tm else tm
    const = lambda i: (0, 0)
    once = pl.Buffered(1)
    if seq_len >= tm:
        r = tm // SUBLANES
        prev_map_c = lambda i: (jnp.maximum(i * r - 1, 0), 1)
        prev_map_h = lambda i: (jnp.maximum(i * r - 1, 0), 2)
        hist_specs = [pl.BlockSpec((SUBLANES, D_CONV), prev_map_c), pl.BlockSpec((SUBLANES, D_CONV), prev_map_h)]
        hist_args = (conv3, conv3)
    else:
        hist_specs = [pl.BlockSpec((tm, D_CONV), lambda i: (i, 0))] * 2
        hist_args = prev_or_hist
    in_specs = ([pl.BlockSpec((tm, D_CONV), lambda i: (i, 0)),
                 pl.BlockSpec((tm, D_CONV), lambda i: (i, 1)),
                 pl.BlockSpec((tm, D_CONV), lambda i: (i, 2))]
                + hist_specs
                + [pl.BlockSpec((tm, D_ATTN), lambda i: (i, 0)),
                   pl.BlockSpec((tm, D_MODEL), lambda i: (i, 0)),
                   pl.BlockSpec((tm, D_MODEL), lambda i: (i, 1)),
                   pl.BlockSpec((tm, D_MODEL), lambda i: (i, 0)),
                   pl.BlockSpec((SUBLANES, D_CONV), const),
                   pl.BlockSpec((D_CONV, D_MODEL), const, pipeline_mode=once),
                   pl.BlockSpec((D_ATTN, D_MODEL), const, pipeline_mode=once),
                   pl.BlockSpec((D_MODEL, D_MODEL), const, pipeline_mode=once),
                   pl.BlockSpec((1, D_MODEL), const),
                   pl.BlockSpec((D_MODEL, LANES), const),
                   pl.BlockSpec((1, LANES), const)])
    return pl.pallas_call(
        functools.partial(_merge_kernel, tm=tm, seq_len=seq_len, tail=tail),
        grid=(n_i,),
        in_specs=in_specs,
        out_specs=[pl.BlockSpec((tm, D_MODEL), lambda i: (i, 0)),
                   pl.BlockSpec((tm, D_MODEL), lambda i: (i, 0)),
                   pl.BlockSpec((tm, LANES), lambda i: (i, 0)),
                   pl.BlockSpec((tm, LANES), lambda i: (i, 0)),
                   pl.BlockSpec((tail, D_CONV), lambda i: (i, 0))],
        out_shape=[jax.ShapeDtypeStruct((t, D_MODEL), F32),
                   jax.ShapeDtypeStruct((t, D_MODEL), F32),
                   jax.ShapeDtypeStruct((t, LANES), jnp.int32),
                   jax.ShapeDtypeStruct((t, LANES), F32),
                   jax.ShapeDtypeStruct((n_i * tail, D_CONV), F32)],
        compiler_params=_cparams(("parallel",)),
        name="merge",
    )(conv3, conv3, conv3, *hist_args, ya, gates, gates, x, conv_w, wc, wa, wo, gn, wr, rb)


MOE_TM = 256
MOE_TF = 1024
MOE_TN = 1024


def _prefetched_weights(sched, j, i, n_passes, copies, wbuf_ref, wb_ref):
    be_ref, first_ref, nxt_ref, gidx_ref, s_ref = sched
    n_groups = s_ref[1]

    @pl.when(first_ref[i] == 1)
    def _():
        @pl.when((j == 0) & (gidx_ref[i] == 0))
        def _():
            for c in copies(be_ref[i], j):
                c.start()

        for c in copies(be_ref[i], j):
            c.wait()
        wb_ref[...] = wbuf_ref[...].astype(BF16)
        j_next = jnp.where(gidx_ref[i] == n_groups - 1, j + 1, j)

        @pl.when(j_next < n_passes)
        def _():
            for c in copies(nxt_ref[i], j_next):
                c.start()


def _moe_up_kernel(*refs):
    sched, (x_ref, w_ref, bg_ref, bl_ref, o_ref, wbuf_ref, wb_ref, sem_ref) = refs[:5], refs[5:]
    j, i = pl.program_id(0), pl.program_id(1)

    def copies(expert, jc):
        return [pltpu.make_async_copy(
            w_ref.at[expert, :, pl.ds(pl.multiple_of(half * D_FF + jc * MOE_TF, MOE_TF), MOE_TF)],
            wbuf_ref.at[half], sem_ref.at[0]) for half in range(2)]

    _prefetched_weights(sched, j, i, pl.num_programs(0), copies, wbuf_ref, wb_ref)

    @pl.when(i < sched[4][0])
    def _():
        x = x_ref[...]
        h_glu = jnp.minimum(_dot(x, wb_ref[0]) + bg_ref[0], SWIGLU_LIMIT)
        h_lin = jnp.clip(_dot(x, wb_ref[1]) + bl_ref[0], -SWIGLU_LIMIT, SWIGLU_LIMIT)
        o_ref[...] = (h_glu * jax.nn.sigmoid(SWIGLU_ALPHA * h_glu) * (h_lin + 1.0)).astype(BF16)

    @pl.when(i >= sched[4][0])
    def _():
        o_ref[...] = jnp.zeros_like(o_ref)


def _moe_down_kernel(*refs):
    sched, (a_ref, w_ref, b_ref, o_ref, wbuf_ref, wb_ref, sem_ref) = refs[:5], refs[5:]
    j, i = pl.program_id(0), pl.program_id(1)

    def copies(expert, jc):
        return [pltpu.make_async_copy(w_ref.at[expert, :, pl.ds(pl.multiple_of(jc * MOE_TN, MOE_TN), MOE_TN)],
                                      wbuf_ref, sem_ref.at[0])]

    _prefetched_weights(sched, j, i, pl.num_programs(0), copies, wbuf_ref, wb_ref)

    @pl.when(i < sched[4][0])
    def _():
        o_ref[...] = _dot(a_ref[...], wb_ref[...]) + b_ref[0]

    @pl.when(i >= sched[4][0])
    def _():
        o_ref[...] = jnp.zeros_like(o_ref)


def _moe_experts(xs, sched, w1, b1, w2, b2):
    n_rows = xs.shape[0]
    n_i = n_rows // MOE_TM
    n_f = D_FF // MOE_TF
    b1r = b1.reshape(N_EXPERTS, 1, 2 * D_FF)
    b2r = b2.reshape(N_EXPERTS, 1, D_MODEL)
    hbm = pl.BlockSpec(memory_space=pl.ANY)
    act = pl.pallas_call(
        _moe_up_kernel,
        grid_spec=pltpu.PrefetchScalarGridSpec(
            num_scalar_prefetch=5,
            grid=(n_f, n_i),
            in_specs=[pl.BlockSpec((MOE_TM, D_MODEL), lambda j, i, be, *_: (i, 0)),
                      hbm,
                      pl.BlockSpec((1, 1, MOE_TF), lambda j, i, be, *_: (be[i], 0, j)),
                      pl.BlockSpec((1, 1, MOE_TF), lambda j, i, be, *_: (be[i], 0, n_f + j))],
            out_specs=pl.BlockSpec((MOE_TM, MOE_TF), lambda j, i, be, *_: (i, j)),
            scratch_shapes=[pltpu.VMEM((2, D_MODEL, MOE_TF), F32), pltpu.VMEM((2, D_MODEL, MOE_TF), BF16),
                            pltpu.SemaphoreType.DMA((1,))]),
        out_shape=jax.ShapeDtypeStruct((n_rows, D_FF), BF16),
        compiler_params=_cparams(("arbitrary", "arbitrary")),
        name="moe_up",
    )(*sched, xs, w1, b1r, b1r)
    return pl.pallas_call(
        _moe_down_kernel,
        grid_spec=pltpu.PrefetchScalarGridSpec(
            num_scalar_prefetch=5,
            grid=(D_MODEL // MOE_TN, n_i),
            in_specs=[pl.BlockSpec((MOE_TM, D_FF), lambda j, i, be, *_: (i, 0)),
                      hbm,
                      pl.BlockSpec((1, 1, MOE_TN), lambda j, i, be, *_: (be[i], 0, j))],
            out_specs=pl.BlockSpec((MOE_TM, MOE_TN), lambda j, i, be, *_: (i, j)),
            scratch_shapes=[pltpu.VMEM((D_FF, MOE_TN), F32), pltpu.VMEM((D_FF, MOE_TN), BF16),
                            pltpu.SemaphoreType.DMA((1,))]),
        out_shape=jax.ShapeDtypeStruct((n_rows, D_MODEL), F32),
        compiler_params=_cparams(("arbitrary", "arbitrary")),
        name="moe_down",
    )(*sched, act, w2, b2r)


GATHER_UNROLL = 8


def _gather_rows_kernel(tok_ref, src_ref, o_ref, buf_ref, sem_ref):
    i = pl.program_id(0)

    def for_rows(tile, slot, fn):
        def body(r, c):
            fn(pltpu.make_async_copy(src_ref.at[pl.ds(tok_ref[tile * MOE_TM + r], 1)],
                                     buf_ref.at[slot, pl.ds(r, 1)], sem_ref.at[slot]))
            return c
        lax.fori_loop(0, MOE_TM, body, 0, unroll=GATHER_UNROLL)

    slot = i % 2

    @pl.when(i == 0)
    def _():
        for_rows(0, 0, lambda c: c.start())

    @pl.when(i + 1 < pl.num_programs(0))
    def _():
        for_rows(i + 1, 1 - slot, lambda c: c.start())

    for_rows(i, slot, lambda c: c.wait())
    o_ref[...] = buf_ref[slot].astype(BF16)


def _gather_rows(src, tok):
    n_rows = tok.shape[0]
    d = src.shape[1]
    return pl.pallas_call(
        _gather_rows_kernel,
        grid_spec=pltpu.PrefetchScalarGridSpec(
            num_scalar_prefetch=1, grid=(n_rows // MOE_TM,),
            in_specs=[pl.BlockSpec(memory_space=pl.ANY)],
            out_specs=pl.BlockSpec((MOE_TM, d), lambda i, tok: (i, 0)),
            scratch_shapes=[pltpu.VMEM((2, MOE_TM, d), F32), pltpu.SemaphoreType.DMA((2,))]),
        out_shape=jax.ShapeDtypeStruct((n_rows, d), BF16),
        compiler_params=_cparams(("arbitrary",)),
        name="gather_rows",
    )(tok, src)


def _route(top_i):
    t = top_i.shape[0]
    n_assign = t * TOP_K
    flat_e = top_i.reshape(-1)
    experts = jnp.arange(N_EXPERTS, dtype=jnp.int32)
    order = jnp.argsort(flat_e).astype(jnp.int32)
    counts = jnp.sum((flat_e[:, None] == experts[None, :]).astype(jnp.int32), axis=0)
    offsets = jnp.cumsum(counts) - counts
    padded = (counts + MOE_TM - 1) // MOE_TM * MOE_TM
    padded_end = jnp.cumsum(padded)
    padded_off = padded_end - padded
    n_tiles = -(-n_assign // MOE_TM) + N_EXPERTS
    tile_start = jnp.arange(n_tiles, dtype=jnp.int32) * MOE_TM
    block_expert = jnp.minimum(jnp.sum((tile_start[:, None] >= padded_end[None, :]).astype(jnp.int32), axis=1),
                               N_EXPERTS - 1)
    in_group = (tile_start - padded_off[block_expert])[:, None] + jnp.arange(MOE_TM, dtype=jnp.int32)[None, :]
    valid = in_group < counts[block_expert][:, None]
    sorted_idx = jnp.clip(offsets[block_expert][:, None] + in_group, 0, n_assign - 1)
    row_tok = jnp.where(valid, order[sorted_idx] // TOP_K, t).reshape(-1)
    sorted_e = flat_e[order]
    dest = padded_off[sorted_e] + (jnp.arange(n_assign, dtype=jnp.int32) - offsets[sorted_e])
    _, pos = lax.sort((order, dest), num_keys=1)
    tiles = jnp.arange(n_tiles, dtype=jnp.int32)
    first = jnp.concatenate([jnp.ones((1,), jnp.bool_), block_expert[1:] != block_expert[:-1]])
    group = jnp.cumsum(first.astype(jnp.int32)) - 1
    next_first = lax.cummin(jnp.where(first, tiles, n_tiles), reverse=True)
    next_first = jnp.concatenate([next_first[1:], jnp.full((1,), n_tiles, jnp.int32)])
    next_expert = jnp.where(next_first < n_tiles, block_expert[jnp.minimum(next_first, n_tiles - 1)],
                            block_expert[0])
    scalars = jnp.stack([padded_end[-1] // MOE_TM, group[-1] + 1]).astype(jnp.int32)
    sched = (block_expert, first.astype(jnp.int32), next_expert, group, scalars)
    return row_tok, pos.reshape(t, TOP_K), sched


COMBINE_UNROLL = 8


def _combine_kernel(pos_ref, h_ref, g_ref, gn_ref, ys_ref, o_ref, buf_ref, sem_ref, *, tm):
    i = pl.program_id(0)
    n_i = pl.num_programs(0)

    def row_copy(tile, slot, r, k):
        src = pos_ref[(tile * tm + r) * TOP_K + k]
        return pltpu.make_async_copy(ys_ref.at[pl.ds(src, 1)], buf_ref.at[slot, k, pl.ds(r, 1)], sem_ref.at[slot])

    def for_rows(tile, slot, fn):
        def body(r, c):
            for k in range(TOP_K):
                fn(row_copy(tile, slot, r, k))
            return c
        lax.fori_loop(0, tm, body, 0, unroll=COMBINE_UNROLL)

    slot = i % 2

    @pl.when(i == 0)
    def _():
        for_rows(0, 0, lambda c: c.start())

    @pl.when(i + 1 < n_i)
    def _():
        for_rows(i + 1, 1 - slot, lambda c: c.start())

    for_rows(i, slot, lambda c: c.wait())
    y = h_ref[...]
    g = g_ref[...]
    for k in range(TOP_K):
        y = y + g[:, k:k + 1] * buf_ref[slot, k]
    inv = lax.rsqrt(jnp.mean(y * y, axis=-1, keepdims=True) + RMS_EPS)
    o_ref[...] = y * inv * gn_ref[...]


def _combine(h, ys, pos, g, gn, *, tm):
    t = h.shape[0]
    return pl.pallas_call(
        functools.partial(_combine_kernel, tm=tm),
        grid_spec=pltpu.PrefetchScalarGridSpec(
            num_scalar_prefetch=1,
            grid=(t // tm,),
            in_specs=[pl.BlockSpec((tm, D_MODEL), lambda i, pos: (i, 0)),
                      pl.BlockSpec((tm, LANES), lambda i, pos: (i, 0)),
                      pl.BlockSpec((1, D_MODEL), lambda i, pos: (0, 0)),
                      pl.BlockSpec(memory_space=pl.ANY)],
            out_specs=pl.BlockSpec((tm, D_MODEL), lambda i, pos: (i, 0)),
            scratch_shapes=[pltpu.VMEM((2, TOP_K, tm, D_MODEL), F32), pltpu.SemaphoreType.DMA((2,))]),
        out_shape=jax.ShapeDtypeStruct((t, D_MODEL), F32),
        compiler_params=_cparams(("arbitrary",)),
        name="combine",
    )(pos.reshape(-1), h, g, gn, ys)


def kernel(x_prompt, x_sample, cache_k, cache_v, state_conv, page_table, norm_mix, w_in, conv_w, sb_bias,
           w_br_conv, w_br_attn, w_o, norm_ffn, router_w, router_b, w_mlp1, b_mlp1, w_mlp2, b_mlp2, norm_final):
    assert w_in.shape[0] == 1, "single layer"
    batch, seq, d = x_prompt.shape
    dec_b, dec_t, _ = x_sample.shape
    n_pool, page = cache_k.shape[1], cache_k.shape[2]
    t_p, t_s = batch * seq, dec_b * dec_t
    assert dec_t > CONV_W - 1

    g_mix = norm_mix[0].reshape(1, d)
    g_ffn = norm_ffn[0].reshape(1, d)
    g_fin = norm_final.reshape(1, d)
    cw = jnp.pad(conv_w[0], ((0, SUBLANES - CONV_W), (0, 0)))
    wr = jnp.pad(router_w[0], ((0, 0), (0, LANES - N_EXPERTS)))
    rb = jnp.pad(router_b[0].astype(F32), (0, LANES - N_EXPERTS), constant_values=NEG_BIG).reshape(1, LANES)

    w_in_b = w_in[0].astype(BF16)
    weights = (w_br_conv[0].astype(BF16), w_br_attn[0].astype(BF16), w_o[0].astype(BF16), g_ffn,
               wr.astype(BF16), rb)

    xp = x_prompt.reshape(t_p, d)
    conv3, q, k, v, gates = _inproj(xp, g_mix, w_in_b, tm=1024)
    ya = _attn_prompt(q, k, v, sb_bias[0], batch, seq)
    h_p, hn_p, ti_p, tg_p, ut_p = _merge(conv3, None, ya, gates, xp, cw, *weights, tm=256, seq_len=seq)

    xs = x_sample.reshape(t_s, d)
    conv3_s, q_s, k_s, v_s, gates_s = _inproj(xs, g_mix, w_in_b, tm=t_s)
    new_rows = (dec_b, dec_t, N_HEADS, HEAD_DIM)
    ya_s = _attn_sample(q_s.reshape(new_rows), k_s.reshape(new_rows), v_s.reshape(new_rows),
                        cache_k[0], cache_v[0], page_table, sb_bias[0])
    hist = state_conv[0].astype(F32)
    zero = jnp.zeros((dec_b, dec_t - 2, D_CONV), F32)
    hist1 = jnp.concatenate([hist[:, 1:2], jnp.zeros((dec_b, dec_t - 1, D_CONV), F32)], axis=1)
    hist2 = jnp.concatenate([hist, zero], axis=1)
    h_s, hn_s, ti_s, tg_s, u_s = _merge(
        conv3_s, (hist1.reshape(t_s, D_CONV), hist2.reshape(t_s, D_CONV)), ya_s.reshape(t_s, D_ATTN),
        gates_s, xs, cw, *weights, tm=t_s, seq_len=dec_t)

    t_all = t_p + t_s
    top_i = jnp.concatenate([ti_p[:, :TOP_K], ti_s[:, :TOP_K]], axis=0)
    row_tok, pos, sched = _route(top_i)
    hn_all = jnp.concatenate([hn_p, hn_s, jnp.zeros((1, d), F32)], axis=0)
    ys = _moe_experts(_gather_rows(hn_all, row_tok), sched, w_mlp1[0], b_mlp1[0], w_mlp2[0], b_mlp2[0])
    y_p = _combine(h_p, ys, pos[:t_p], tg_p, g_fin, tm=128)
    y_s = _combine(h_s, ys, pos[t_p:], tg_s, g_fin, tm=t_s)

    n_hist = CONV_W - 1
    conv_p = ut_p.reshape(batch, seq // 256, SUBLANES, D_CONV)[:, -1, SUBLANES - n_hist:]
    conv_s = u_s.reshape(dec_b, dec_t, D_CONV)[:, dec_t - n_hist:]
    return (y_p.reshape(batch, seq, d), y_s.reshape(dec_b, dec_t, d),
            k.reshape(1, batch, seq, N_HEADS, HEAD_DIM), v.reshape(1, batch, seq, N_HEADS, HEAD_DIM),
            conv_p[None], k_s.reshape(1, dec_b, dec_t, N_HEADS, HEAD_DIM),
            v_s.reshape(1, dec_b, dec_t, N_HEADS, HEAD_DIM), conv_s[None])
```

```python
import functools

import jax
import jax.numpy as jnp
from jax import lax
from jax.experimental import pallas as pl
from jax.experimental.pallas import tpu as pltpu

F32 = jnp.float32
BF16 = jnp.bfloat16

D_MODEL = 2048
N_HEADS = 16
HEAD_DIM = 64
D_ATTN = N_HEADS * HEAD_DIM
D_CONV = D_MODEL // 2
CONV_W = 3
N_EXPERTS = 32
TOP_K = 4
D_FF = D_MODEL
SWIGLU_LIMIT = 7.0
SWIGLU_ALPHA = 1.702
RMS_EPS = 1e-5
D_IN = 3 * D_CONV + 3 * D_ATTN + 2 * D_MODEL

LANES = 128
SUBLANES = 8
VMEM_LIMIT = 56 * 1024 * 1024
NEG_BIG = -3.0e38


def _cparams(sem):
    return pltpu.CompilerParams(dimension_semantics=sem, vmem_limit_bytes=VMEM_LIMIT)


def _dot(a, b):
    return jnp.dot(a.astype(BF16), b.astype(BF16), preferred_element_type=F32)


IN_TN = 512
_IN_GROUPS = ((0, 3 * D_CONV // IN_TN),
              (3 * D_CONV // IN_TN, D_ATTN // IN_TN),
              ((3 * D_CONV + D_ATTN) // IN_TN, D_ATTN // IN_TN),
              ((3 * D_CONV + 2 * D_ATTN) // IN_TN, D_ATTN // IN_TN),
              ((3 * D_CONV + 3 * D_ATTN) // IN_TN, 2 * D_MODEL // IN_TN))


def _inproj_kernel(x_ref, g_ref, w_ref, *refs):
    out_refs, xn_ref = refs[:-1], refs[-1]
    j = pl.program_id(1)

    @pl.when(j == 0)
    def _():
        x = x_ref[...]
        inv = lax.rsqrt(jnp.mean(x * x, axis=-1, keepdims=True) + RMS_EPS)
        xn_ref[...] = (x * inv * g_ref[...]).astype(xn_ref.dtype)

    acc = _dot(xn_ref[...], w_ref[...])
    for o_ref, (a, n) in zip(out_refs, _IN_GROUPS):
        @pl.when((j >= a) & (j < a + n))
        def _(o_ref=o_ref):
            o_ref[...] = acc


def _inproj(x, g, w, *, tm):
    t = x.shape[0]
    n_j = D_IN // IN_TN

    def omap(a, n):
        return lambda i, j: (i, jnp.clip(j - a, 0, n - 1))

    return pl.pallas_call(
        _inproj_kernel,
        grid=(t // tm, n_j),
        in_specs=[pl.BlockSpec((tm, D_MODEL), lambda i, j: (i, 0)),
                  pl.BlockSpec((1, D_MODEL), lambda i, j: (0, 0)),
                  pl.BlockSpec((D_MODEL, IN_TN), lambda i, j: (0, j))],
        out_specs=[pl.BlockSpec((tm, IN_TN), omap(a, n)) for a, n in _IN_GROUPS],
        out_shape=[jax.ShapeDtypeStruct((t, n * IN_TN), F32) for _, n in _IN_GROUPS],
        scratch_shapes=[pltpu.VMEM((tm, D_MODEL), BF16)],
        compiler_params=_cparams(("parallel", "arbitrary")),
        name="inproj",
    )(x, g, w)


LOG2E = 1.4426950408889634
SP_LINEAR_ABOVE = 64.0


def _suffix_matrix(tk):
    m = lax.broadcasted_iota(jnp.int32, (2 * tk, tk), 0) % tk
    j = lax.broadcasted_iota(jnp.int32, (2 * tk, tk), 1)
    return (m > j).astype(BF16)


def _sb_scores(z, u2, mask):
    sp = jnp.where(z > SP_LINEAR_ABOVE, z, jnp.log(1.0 + jnp.exp2(z)) * LOG2E)
    log_beta = z - sp
    if mask is not None:
        sp = jnp.where(mask, sp, 0.0)
    hi = pltpu.bitcast(pltpu.bitcast(sp, jnp.int32) & jnp.int32(-65536), F32)
    lo = sp - hi
    hl = jnp.concatenate([hi.astype(BF16), lo.astype(BF16)], axis=1)
    half = hl.shape[0] // 2
    s = jnp.concatenate([jnp.dot(hl[:half], u2, preferred_element_type=F32),
                         jnp.dot(hl[half:], u2, preferred_element_type=F32)], axis=0)
    w = jnp.exp2(log_beta - s)
    if mask is not None:
        w = jnp.where(mask, w, 0.0)
    return w.astype(BF16), s[:, 0:1] + sp[:, 0:1]


ATT_T = 256


def _attn_prompt_kernel(q_ref, k_ref, v_ref, bias_ref, u2_ref, o_ref, kb_ref, vb_ref, acc_ref):
    qi = pl.program_id(2)
    t = ATT_T

    @pl.when(qi == 0)
    def _():
        kb_ref[...] = k_ref[...].astype(BF16)
        vb_ref[...] = v_ref[...].astype(BF16)

    q = q_ref[...] * (HEAD_DIM ** -0.5 * LOG2E)
    lane = lax.broadcasted_iota(jnp.int32, (t, LANES), 1)
    q2 = jnp.concatenate([jnp.where(lane < HEAD_DIM, q, 0.0), jnp.where(lane >= HEAD_DIM, q, 0.0)],
                         axis=0).astype(BF16)
    bias = bias_ref[0]
    u2 = u2_ref[...]

    def logits(j):
        kb = kb_ref[pl.ds(pl.multiple_of(j * t, t), t), :]
        return lax.dot_general(q2, kb, (((1,), (1,)), ((), ())), preferred_element_type=F32) + bias

    def accumulate(j, w, carry):
        vb = vb_ref[pl.ds(pl.multiple_of(j * t, t), t), :]
        acc_ref[...] += jnp.exp2(-carry) * jnp.dot(w, vb, preferred_element_type=F32)

    acc_ref[...] = jnp.zeros_like(acc_ref)
    row = lax.broadcasted_iota(jnp.int32, (2 * t, t), 0) % t
    col = lax.broadcasted_iota(jnp.int32, (2 * t, t), 1)
    w, tot = _sb_scores(logits(qi), u2, col < row)
    z = logits(jnp.maximum(qi - 1, 0))

    def body(n, c):
        z, w, tot, carry = c
        z_next = logits(jnp.maximum(qi - n - 2, 0))
        nxt = _sb_scores(z, u2, None)
        accumulate(qi - n, w, carry)
        return (z_next,) + nxt + (carry + tot,)

    _, w, _, carry = lax.fori_loop(0, qi, body, (z, w, tot, jnp.zeros((2 * t, 1), F32)))
    accumulate(0, w, carry)
    acc = acc_ref[...]
    o_ref[...] = jnp.where(lane < HEAD_DIM, acc[:t], acc[t:])


def _attn_prompt(q, k, v, sb_bias, batch, seq):
    t = ATT_T
    n_q = seq // t
    n_hp = N_HEADS // 2
    bias = jnp.broadcast_to(jnp.repeat(sb_bias.astype(F32).reshape(n_hp, 2) * LOG2E, t, axis=1)[:, :, None],
                            (n_hp, 2 * t, t))
    return pl.pallas_call(
        _attn_prompt_kernel,
        grid=(batch, n_hp, n_q),
        in_specs=[pl.BlockSpec((t, LANES), lambda b, h, i: (b * n_q + i, h)),
                  pl.BlockSpec((seq, LANES), lambda b, h, i: (b, h)),
                  pl.BlockSpec((seq, LANES), lambda b, h, i: (b, h)),
                  pl.BlockSpec((1, 2 * t, t), lambda b, h, i: (h, 0, 0)),
                  pl.BlockSpec((2 * t, t), lambda b, h, i: (0, 0))],
        out_specs=pl.BlockSpec((t, LANES), lambda b, h, i: (b * n_q + i, h)),
        out_shape=jax.ShapeDtypeStruct((batch * seq, D_ATTN), F32),
        scratch_shapes=[pltpu.VMEM((seq, LANES), BF16), pltpu.VMEM((seq, LANES), BF16),
                        pltpu.VMEM((2 * t, LANES), F32)],
        compiler_params=_cparams(("parallel", "parallel", "arbitrary")),
        name="attn_prompt",
    )(q, k, v, bias, _suffix_matrix(t))


SAMPLE_QR = 16
SAMPLE_PAGES_PER_STEP = 8


def _attn_sample_kernel(pt_ref, q_ref, kn_ref, vn_ref, *refs, page, pages_per_step):
    kc_refs, vc_refs = refs[:pages_per_step], refs[pages_per_step:2 * pages_per_step]
    bias_ref, u2_ref, o_ref, qb_ref, acc_ref, carry_ref = refs[2 * pages_per_step:]
    p = pl.program_id(1)
    qr = SAMPLE_QR
    u2 = u2_ref[...]

    def head_slab(ref, h):
        return ref[h * HEAD_DIM:(h + 1) * HEAD_DIM, :].astype(BF16)

    def scores(k_ref, mask):
        z = jnp.concatenate(
            [jnp.dot(qb_ref[h * qr:(h + 1) * qr, :], head_slab(k_ref, h), preferred_element_type=F32)
             for h in range(N_HEADS)], axis=0) * (HEAD_DIM ** -0.5 * LOG2E) + bias_ref[...]
        return _sb_scores(z, u2, mask)

    def sweep(k_refs, v_refs, mask):
        blocks = [scores(k_ref, mask) for k_ref in k_refs]
        carry = carry_ref[...]
        acc = [acc_ref[h * qr:(h + 1) * qr, :] for h in range(N_HEADS)]
        for (w, tot), v_ref in zip(blocks, v_refs):
            scale = jnp.exp2(-carry)
            carry = carry + tot
            for h in range(N_HEADS):
                rows = slice(h * qr, (h + 1) * qr)
                acc[h] = acc[h] + scale[rows] * lax.dot_general(
                    w[rows], head_slab(v_ref, h), (((1,), (1,)), ((), ())), preferred_element_type=F32)
        carry_ref[...] = carry
        for h in range(N_HEADS):
            acc_ref[h * qr:(h + 1) * qr, :] = acc[h]

    @pl.when(p == 0)
    def _():
        qb_ref[...] = q_ref[...].astype(BF16)
        carry_ref[...] = jnp.zeros_like(carry_ref)
        acc_ref[...] = jnp.zeros_like(acc_ref)
        qpos = lax.broadcasted_iota(jnp.int32, (N_HEADS * qr, page), 0) % qr
        kpos = lax.broadcasted_iota(jnp.int32, (N_HEADS * qr, page), 1)
        sweep([kn_ref], [vn_ref], kpos < qpos)

    sweep(kc_refs, vc_refs, None)

    @pl.when(p == pl.num_programs(1) - 1)
    def _():
        o_ref[...] = acc_ref[...]


def _attn_sample(q, k, v, cache_k, cache_v, page_table, sb_bias):
    dec_b, dec_t = q.shape[:2]
    n_pool, page = cache_k.shape[:2]
    n_pages = page_table.shape[1]
    pps = SAMPLE_PAGES_PER_STEP
    qr = SAMPLE_QR
    assert page == LANES and dec_t <= qr and n_pages % pps == 0
    rows = N_HEADS * qr
    q_rows = jnp.pad(q.transpose(0, 2, 1, 3), ((0, 0), (0, 0), (0, qr - dec_t), (0, 0))).reshape(dec_b, rows, HEAD_DIM)
    pad = ((0, 0), (0, 0), (0, 0), (0, page - dec_t))
    k_new = jnp.pad(k.transpose(0, 2, 3, 1), pad).reshape(dec_b, D_ATTN, page)
    v_new = jnp.pad(v.transpose(0, 2, 3, 1), pad).reshape(dec_b, D_ATTN, page)
    bias = jnp.broadcast_to(jnp.repeat(sb_bias.astype(F32) * LOG2E, qr)[:, None], (rows, page))
    pool_t = lambda c: c.transpose(0, 2, 3, 1).reshape(n_pool, D_ATTN, page)
    page_block = (None, D_ATTN, page)

    def page_spec(s):
        return pl.BlockSpec(page_block, lambda b, p, pt: (pt[b * n_pages + n_pages - 1 - (p * pps + s)], 0, 0))

    per_seq = lambda b, p, pt: (b, 0, 0)
    out = pl.pallas_call(
        functools.partial(_attn_sample_kernel, page=page, pages_per_step=pps),
        grid_spec=pltpu.PrefetchScalarGridSpec(
            num_scalar_prefetch=1,
            grid=(dec_b, n_pages // pps),
            in_specs=([pl.BlockSpec((None, rows, HEAD_DIM), per_seq),
                       pl.BlockSpec(page_block, per_seq),
                       pl.BlockSpec(page_block, per_seq)]
                      + [page_spec(s) for s in range(pps)] * 2
                      + [pl.BlockSpec((rows, page), lambda b, p, pt: (0, 0)),
                         pl.BlockSpec((2 * page, page), lambda b, p, pt: (0, 0))]),
            out_specs=pl.BlockSpec((None, rows, HEAD_DIM), per_seq),
            scratch_shapes=[pltpu.VMEM((rows, HEAD_DIM), BF16), pltpu.VMEM((rows, HEAD_DIM), F32),
                            pltpu.VMEM((rows, 1), F32)]),
        out_shape=jax.ShapeDtypeStruct((dec_b, rows, HEAD_DIM), F32),
        compiler_params=_cparams(("parallel", "arbitrary")),
        name="attn_sample",
    )(page_table.reshape(-1), q_rows, k_new, v_new,
      *([pool_t(cache_k)] * pps), *([pool_t(cache_v)] * pps), bias, _suffix_matrix(page))
    return out.reshape(dec_b, N_HEADS, qr, HEAD_DIM)[:, :, :dec_t].transpose(0, 2, 1, 3)


def _top4(logits):
    lane = lax.broadcasted_iota(jnp.int32, logits.shape, 1)
    vals, idxs = [], []
    l = logits
    for _ in range(TOP_K):
        m = jnp.max(l, axis=-1, keepdims=True)
        idx = jnp.min(jnp.where(l == m, lane, LANES), axis=-1, keepdims=True)
        vals.append(m)
        idxs.append(idx)
        l = jnp.where(lane == idx, NEG_BIG, l)
    es = [jnp.exp(v - vals[0]) for v in vals]
    inv = 1.0 / (es[0] + es[1] + es[2] + es[3])
    out_i = jnp.zeros(logits.shape, jnp.int32)
    out_g = jnp.zeros(logits.shape, F32)
    for k in range(TOP_K):
        out_i = jnp.where(lane == k, idxs[k], out_i)
        out_g = jnp.where(lane == k, es[k] * inv, out_g)
    return out_i, out_g


def _merge_kernel(*refs, tm, seq_len, tail):
    if seq_len >= tm:
        cb_ref, cc_ref, ch_ref, pcc_ref, pch_ref = refs[:5]
    else:
        cb_ref, cc_ref, ch_ref, h1_ref, h2_ref = refs[:5]
    (ya_ref, ga_ref, gb_ref, x_ref, cw_ref, wc_ref, wa_ref, wo_ref, gn_ref, wr_ref, rb_ref,
     h_ref, hn_ref, ti_ref, tg_ref, ut_ref) = refs[5:]
    i = pl.program_id(0)

    u = cc_ref[...] * ch_ref[...]
    row = lax.broadcasted_iota(jnp.int32, u.shape, 0)
    if seq_len >= tm:
        at_start = (i % (seq_len // tm)) == 0
        up = jnp.where(at_start, 0.0, pcc_ref[...] * pch_ref[...])
        u1 = jnp.where(row == 0, up[SUBLANES - 1:SUBLANES], pltpu.roll(u, 1, axis=0))
        u2 = jnp.where(row == 0, up[SUBLANES - 2:SUBLANES - 1],
                       jnp.where(row == 1, up[SUBLANES - 1:SUBLANES], pltpu.roll(u, 2, axis=0)))
    else:
        u1 = jnp.where(row % seq_len == 0, h1_ref[...], pltpu.roll(u, 1, axis=0))
        u2 = jnp.where(row % seq_len < 2, h2_ref[...], pltpu.roll(u, 2, axis=0))
    ut_ref[...] = u[tm - tail:]
    cw = cw_ref[...]
    y_conv = cb_ref[...] * (cw[0:1] * u2 + cw[1:2] * u1 + cw[2:3] * u)

    merged = (jax.nn.sigmoid(ga_ref[...]) * _dot(y_conv, wc_ref[...])
              + jax.nn.sigmoid(gb_ref[...]) * _dot(ya_ref[...], wa_ref[...]))
    h = x_ref[...] + _dot(merged, wo_ref[...])
    h_ref[...] = h
    inv = lax.rsqrt(jnp.mean(h * h, axis=-1, keepdims=True) + RMS_EPS)
    hn = h * inv * gn_ref[...]
    hn_ref[...] = hn
    logits = _dot(hn, wr_ref[...]) + rb_ref[...]
    ti, tg = _top4(logits)
    ti_ref[...] = ti
    tg_ref[...] = tg


def _merge(conv3, prev_or_hist, ya, gates, x, conv_w, wc, wa, wo, gn, wr, rb, *, tm, seq_len):
    t = x.shape[0]
    n_i = t // tm
    tail = min(tm, SUBLANES) if seq_len >= tm else tm
    const = lambda i: (0, 0)
    once = pl.Buffered(1)
    if seq_len >= tm:
        r = tm // SUBLANES
        prev_map_c = lambda i: (jnp.maximum(i * r - 1, 0), 1)
        prev_map_h = lambda i: (jnp.maximum(i * r - 1, 0), 2)
        hist_specs = [pl.BlockSpec((SUBLANES, D_CONV), prev_map_c), pl.BlockSpec((SUBLANES, D_CONV), prev_map_h)]
        hist_args = (conv3, conv3)
    else:
        hist_specs = [pl.BlockSpec((tm, D_CONV), lambda i: (i, 0))] * 2
        hist_args = prev_or_hist
    in_specs = ([pl.BlockSpec((tm, D_CONV), lambda i: (i, 0)),
                 pl.BlockSpec((tm, D_CONV), lambda i: (i, 1)),
                 pl.BlockSpec((tm, D_CONV), lambda i: (i, 2))]
                + hist_specs
                + [pl.BlockSpec((tm, D_ATTN), lambda i: (i, 0)),
                   pl.BlockSpec((tm, D_MODEL), lambda i: (i, 0)),
                   pl.BlockSpec((tm, D_MODEL), lambda i: (i, 1)),
                   pl.BlockSpec((tm, D_MODEL), lambda i: (i, 0)),
                   pl.BlockSpec((SUBLANES, D_CONV), const),
                   pl.BlockSpec((D_CONV, D_MODEL), const, pipeline_mode=once),
                   pl.BlockSpec((D_ATTN, D_MODEL), const, pipeline_mode=once),
                   pl.BlockSpec((D_MODEL, D_MODEL), const, pipeline_mode=once),
                   pl.BlockSpec((1, D_MODEL), const),
                   pl.BlockSpec((D_MODEL, LANES), const),
                   pl.BlockSpec((1, LANES), const)])
    return pl.pallas_call(
        functools.partial(_merge_kernel, tm=tm, seq_len=seq_len, tail=tail),
        grid=(n_i,),
        in_specs=in_specs,
        out_specs=[pl.BlockSpec((tm, D_MODEL), lambda i: (i, 0)),
                   pl.BlockSpec((tm, D_MODEL), lambda i: (i, 0)),
                   pl.BlockSpec((tm, LANES), lambda i: (i, 0)),
                   pl.BlockSpec((tm, LANES), lambda i: (i, 0)),
                   pl.BlockSpec((tail, D_CONV), lambda i: (i, 0))],
        out_shape=[jax.ShapeDtypeStruct((t, D_MODEL), F32),
                   jax.ShapeDtypeStruct((t, D_MODEL), F32),
                   jax.ShapeDtypeStruct((t, LANES), jnp.int32),
                   jax.ShapeDtypeStruct((t, LANES), F32),
                   jax.ShapeDtypeStruct((n_i * tail, D_CONV), F32)],
        compiler_params=_cparams(("parallel",)),
        name="merge",
    )(conv3, conv3, conv3, *hist_args, ya, gates, gates, x, conv_w, wc, wa, wo, gn, wr, rb)


MOE_TM = 256
MOE_TF = 1024
MOE_TN = 2048


def _prefetched_weights(sched, j, i, n_passes, copies, wbuf_ref, wb_ref):
    be_ref, first_ref, nxt_ref, gidx_ref, s_ref = sched
    n_groups = s_ref[1]

    @pl.when(first_ref[i] == 1)
    def _():
        @pl.when((j == 0) & (gidx_ref[i] == 0))
        def _():
            for c in copies(be_ref[i], j):
                c.start()

        for c in copies(be_ref[i], j):
            c.wait()
        wb_ref[...] = wbuf_ref[...].astype(BF16)
        j_next = jnp.where(gidx_ref[i] == n_groups - 1, j + 1, j)

        @pl.when(j_next < n_passes)
        def _():
            for c in copies(nxt_ref[i], j_next):
                c.start()


def _moe_up_kernel(*refs):
    sched, (x_ref, w_ref, bg_ref, bl_ref, o_ref, wbuf_ref, wb_ref, sem_ref) = refs[:5], refs[5:]
    j, i = pl.program_id(0), pl.program_id(1)

    def copies(expert, jc):
        return [pltpu.make_async_copy(
            w_ref.at[expert, :, pl.ds(pl.multiple_of(half * D_FF + jc * MOE_TF, MOE_TF), MOE_TF)],
            wbuf_ref.at[half], sem_ref.at[0]) for half in range(2)]

    _prefetched_weights(sched, j, i, pl.num_programs(0), copies, wbuf_ref, wb_ref)

    @pl.when(i < sched[4][0])
    def _():
        x = x_ref[...]
        h_glu = jnp.minimum(_dot(x, wb_ref[0]) + bg_ref[0], SWIGLU_LIMIT)
        h_lin = jnp.clip(_dot(x, wb_ref[1]) + bl_ref[0], -SWIGLU_LIMIT, SWIGLU_LIMIT)
        o_ref[...] = (h_glu * jax.nn.sigmoid(SWIGLU_ALPHA * h_glu) * (h_lin + 1.0)).astype(BF16)

    @pl.when(i >= sched[4][0])
    def _():
        o_ref[...] = jnp.zeros_like(o_ref)


def _moe_down_kernel(*refs):
    sched, (a_ref, w_ref, b_ref, o_ref, wbuf_ref, wb_ref, sem_ref) = refs[:5], refs[5:]
    j, i = pl.program_id(0), pl.program_id(1)

    def copies(expert, jc):
        return [pltpu.make_async_copy(w_ref.at[expert, :, pl.ds(pl.multiple_of(jc * MOE_TN, MOE_TN), MOE_TN)],
                                      wbuf_ref, sem_ref.at[0])]

    _prefetched_weights(sched, j, i, pl.num_programs(0), copies, wbuf_ref, wb_ref)

    @pl.when(i < sched[4][0])
    def _():
        o_ref[...] = _dot(a_ref[...], wb_ref[...]) + b_ref[0]

    @pl.when(i >= sched[4][0])
    def _():
        o_ref[...] = jnp.zeros_like(o_ref)


def _moe_experts(xs, sched, w1, b1, w2, b2):
    n_rows = xs.shape[0]
    n_i = n_rows // MOE_TM
    n_f = D_FF // MOE_TF
    b1r = b1.reshape(N_EXPERTS, 1, 2 * D_FF)
    b2r = b2.reshape(N_EXPERTS, 1, D_MODEL)
    hbm = pl.BlockSpec(memory_space=pl.ANY)
    act = pl.pallas_call(
        _moe_up_kernel,
        grid_spec=pltpu.PrefetchScalarGridSpec(
            num_scalar_prefetch=5,
            grid=(n_f, n_i),
            in_specs=[pl.BlockSpec((MOE_TM, D_MODEL), lambda j, i, be, *_: (i, 0)),
                      hbm,
                      pl.BlockSpec((1, 1, MOE_TF), lambda j, i, be, *_: (be[i], 0, j)),
                      pl.BlockSpec((1, 1, MOE_TF), lambda j, i, be, *_: (be[i], 0, n_f + j))],
            out_specs=pl.BlockSpec((MOE_TM, MOE_TF), lambda j, i, be, *_: (i, j)),
            scratch_shapes=[pltpu.VMEM((2, D_MODEL, MOE_TF), F32), pltpu.VMEM((2, D_MODEL, MOE_TF), BF16),
                            pltpu.SemaphoreType.DMA((1,))]),
        out_shape=jax.ShapeDtypeStruct((n_rows, D_FF), BF16),
        compiler_params=_cparams(("arbitrary", "arbitrary")),
        name="moe_up",
    )(*sched, xs, w1, b1r, b1r)
    return pl.pallas_call(
        _moe_down_kernel,
        grid_spec=pltpu.PrefetchScalarGridSpec(
            num_scalar_prefetch=5,
            grid=(D_MODEL // MOE_TN, n_i),
            in_specs=[pl.BlockSpec((MOE_TM, D_FF), lambda j, i, be, *_: (i, 0)),
                      hbm,
                      pl.BlockSpec((1, 1, MOE_TN), lambda j, i, be, *_: (be[i], 0, j))],
            out_specs=pl.BlockSpec((MOE_TM, MOE_TN), lambda j, i, be, *_: (i, j)),
            scratch_shapes=[pltpu.VMEM((D_FF, MOE_TN), F32), pltpu.VMEM((D_FF, MOE_TN), BF16),
                            pltpu.SemaphoreType.DMA((1,))]),
        out_shape=jax.ShapeDtypeStruct((n_rows, D_MODEL), F32),
        compiler_params=_cparams(("arbitrary", "arbitrary")),
        name="moe_down",
    )(*sched, act, w2, b2r)


GATHER_UNROLL = 8


def _gather_rows_kernel(tok_ref, src_ref, o_ref, buf_ref, sem_ref):
    i = pl.program_id(0)

    def for_rows(tile, slot, fn):
        def body(r, c):
            fn(pltpu.make_async_copy(src_ref.at[pl.ds(tok_ref[tile * MOE_TM + r], 1)],
                                     buf_ref.at[slot, pl.ds(r, 1)], sem_ref.at[slot]))
            return c
        lax.fori_loop(0, MOE_TM, body, 0, unroll=GATHER_UNROLL)

    slot = i % 2

    @pl.when(i == 0)
    def _():
        for_rows(0, 0, lambda c: c.start())

    @pl.when(i + 1 < pl.num_programs(0))
    def _():
        for_rows(i + 1, 1 - slot, lambda c: c.start())

    for_rows(i, slot, lambda c: c.wait())
    o_ref[...] = buf_ref[slot].astype(BF16)


def _gather_rows(src, tok):
    n_rows = tok.shape[0]
    d = src.shape[1]
    return pl.pallas_call(
        _gather_rows_kernel,
        grid_spec=pltpu.PrefetchScalarGridSpec(
            num_scalar_prefetch=1, grid=(n_rows // MOE_TM,),
            in_specs=[pl.BlockSpec(memory_space=pl.ANY)],
            out_specs=pl.BlockSpec((MOE_TM, d), lambda i, tok: (i, 0)),
            scratch_shapes=[pltpu.VMEM((2, MOE_TM, d), F32), pltpu.SemaphoreType.DMA((2,))]),
        out_shape=jax.ShapeDtypeStruct((n_rows, d), BF16),
        compiler_params=_cparams(("arbitrary",)),
        name="gather_rows",
    )(tok, src)


def _route(top_i):
    t = top_i.shape[0]
    n_assign = t * TOP_K
    flat_e = top_i.reshape(-1)
    experts = jnp.arange(N_EXPERTS, dtype=jnp.int32)
    order = jnp.argsort(flat_e).astype(jnp.int32)
    counts = jnp.sum((flat_e[:, None] == experts[None, :]).astype(jnp.int32), axis=0)
    offsets = jnp.cumsum(counts) - counts
    padded = (counts + MOE_TM - 1) // MOE_TM * MOE_TM
    padded_end = jnp.cumsum(padded)
    padded_off = padded_end - padded
    n_tiles = -(-n_assign // MOE_TM) + N_EXPERTS
    tile_start = jnp.arange(n_tiles, dtype=jnp.int32) * MOE_TM
    block_expert = jnp.minimum(jnp.sum((tile_start[:, None] >= padded_end[None, :]).astype(jnp.int32), axis=1),
                               N_EXPERTS - 1)
    in_group = (tile_start - padded_off[block_expert])[:, None] + jnp.arange(MOE_TM, dtype=jnp.int32)[None, :]
    valid = in_group < counts[block_expert][:, None]
    sorted_idx = jnp.clip(offsets[block_expert][:, None] + in_group, 0, n_assign - 1)
    row_tok = jnp.where(valid, order[sorted_idx] // TOP_K, t).reshape(-1)
    sorted_e = flat_e[order]
    dest = padded_off[sorted_e] + (jnp.arange(n_assign, dtype=jnp.int32) - offsets[sorted_e])
    _, pos = lax.sort((order, dest), num_keys=1)
    tiles = jnp.arange(n_tiles, dtype=jnp.int32)
    first = jnp.concatenate([jnp.ones((1,), jnp.bool_), block_expert[1:] != block_expert[:-1]])
    group = jnp.cumsum(first.astype(jnp.int32)) - 1
    next_first = lax.cummin(jnp.where(first, tiles, n_tiles), reverse=True)
    next_first = jnp.concatenate([next_first[1:], jnp.full((1,), n_tiles, jnp.int32)])
    next_expert = jnp.where(next_first < n_tiles, block_expert[jnp.minimum(next_first, n_tiles - 1)],
                            block_expert[0])
    scalars = jnp.stack([padded_end[-1] // MOE_TM, group[-1] + 1]).astype(jnp.int32)
    sched = (block_expert, first.astype(jnp.int32), next_expert, group, scalars)
    return row_tok, pos.reshape(t, TOP_K), sched


COMBINE_UNROLL = 8


def _combine_kernel(pos_ref, h_ref, g_ref, gn_ref, ys_ref, o_ref, buf_ref, sem_ref, *, tm):
    i = pl.program_id(0)
    n_i = pl.num_programs(0)

    def row_copy(tile, slot, r, k):
        src = pos_ref[(tile * tm + r) * TOP_K + k]
        return pltpu.make_async_copy(ys_ref.at[pl.ds(src, 1)], buf_ref.at[slot, k, pl.ds(r, 1)], sem_ref.at[slot])

    def for_rows(tile, slot, fn):
        def body(r, c):
            for k in range(TOP_K):
                fn(row_copy(tile, slot, r, k))
            return c
        lax.fori_loop(0, tm, body, 0, unroll=COMBINE_UNROLL)

    slot = i % 2

    @pl.when(i == 0)
    def _():
        for_rows(0, 0, lambda c: c.start())

    @pl.when(i + 1 < n_i)
    def _():
        for_rows(i + 1, 1 - slot, lambda c: c.start())

    for_rows(i, slot, lambda c: c.wait())
    y = h_ref[...]
    g = g_ref[...]
    for k in range(TOP_K):
        y = y + g[:, k:k + 1] * buf_ref[slot, k]
    inv = lax.rsqrt(jnp.mean(y * y, axis=-1, keepdims=True) + RMS_EPS)
    o_ref[...] = y * inv * gn_ref[...]


def _combine(h, ys, pos, g, gn, *, tm):
    t = h.shape[0]
    return pl.pallas_call(
        functools.partial(_combine_kernel, tm=tm),
        grid_spec=pltpu.PrefetchScalarGridSpec(
            num_scalar_prefetch=1,
            grid=(t // tm,),
            in_specs=[pl.BlockSpec((tm, D_MODEL), lambda i, pos: (i, 0)),
                      pl.BlockSpec((tm, LANES), lambda i, pos: (i, 0)),
                      pl.BlockSpec((1, D_MODEL), lambda i, pos: (0, 0)),
                      pl.BlockSpec(memory_space=pl.ANY)],
            out_specs=pl.BlockSpec((tm, D_MODEL), lambda i, pos: (i, 0)),
            scratch_shapes=[pltpu.VMEM((2, TOP_K, tm, D_MODEL), F32), pltpu.SemaphoreType.DMA((2,))]),
        out_shape=jax.ShapeDtypeStruct((t, D_MODEL), F32),
        compiler_params=_cparams(("arbitrary",)),
        name="combine",
    )(pos.reshape(-1), h, g, gn, ys)


def kernel(x_prompt, x_sample, cache_k, cache_v, state_conv, page_table, norm_mix, w_in, conv_w, sb_bias,
           w_br_conv, w_br_attn, w_o, norm_ffn, router_w, router_b, w_mlp1, b_mlp1, w_mlp2, b_mlp2, norm_final):
    assert w_in.shape[0] == 1, "single layer"
    batch, seq, d = x_prompt.shape
    dec_b, dec_t, _ = x_sample.shape
    n_pool, page = cache_k.shape[1], cache_k.shape[2]
    t_p, t_s = batch * seq, dec_b * dec_t
    assert dec_t > CONV_W - 1

    g_mix = norm_mix[0].reshape(1, d)
    g_ffn = norm_ffn[0].reshape(1, d)
    g_fin = norm_final.reshape(1, d)
    cw = jnp.pad(conv_w[0], ((0, SUBLANES - CONV_W), (0, 0)))
    wr = jnp.pad(router_w[0], ((0, 0), (0, LANES - N_EXPERTS)))
    rb = jnp.pad(router_b[0].astype(F32), (0, LANES - N_EXPERTS), constant_values=NEG_BIG).reshape(1, LANES)

    w_in_b = w_in[0].astype(BF16)
    weights = (w_br_conv[0].astype(BF16), w_br_attn[0].astype(BF16), w_o[0].astype(BF16), g_ffn,
               wr.astype(BF16), rb)

    xp = x_prompt.reshape(t_p, d)
    conv3, q, k, v, gates = _inproj(xp, g_mix, w_in_b, tm=1024)
    ya = _attn_prompt(q, k, v, sb_bias[0], batch, seq)
    h_p, hn_p, ti_p, tg_p, ut_p = _merge(conv3, None, ya, gates, xp, cw, *weights, tm=256, seq_len=seq)

    xs = x_sample.reshape(t_s, d)
    conv3_s, q_s, k_s, v_s, gates_s = _inproj(xs, g_mix, w_in_b, tm=t_s)
    new_rows = (dec_b, dec_t, N_HEADS, HEAD_DIM)
    ya_s = _attn_sample(q_s.reshape(new_rows), k_s.reshape(new_rows), v_s.reshape(new_rows),
                        cache_k[0], cache_v[0], page_table, sb_bias[0])
    hist = state_conv[0].astype(F32)
    zero = jnp.zeros((dec_b, dec_t - 2, D_CONV), F32)
    hist1 = jnp.concatenate([hist[:, 1:2], jnp.zeros((dec_b, dec_t - 1, D_CONV), F32)], axis=1)
    hist2 = jnp.concatenate([hist, zero], axis=1)
    h_s, hn_s, ti_s, tg_s, u_s = _merge(
        conv3_s, (hist1.reshape(t_s, D_CONV), hist2.reshape(t_s, D_CONV)), ya_s.reshape(t_s, D_ATTN),
        gates_s, xs, cw, *weights, tm=t_s, seq_len=dec_t)

    t_all = t_p + t_s
    top_i = jnp.concatenate([ti_p[:, :TOP_K], ti_s[:, :TOP_K]], axis=0)
    row_tok, pos, sched = _route(top_i)
    hn_all = jnp.concatenate([hn_p, hn_s, jnp.zeros((1, d), F32)], axis=0)
    ys = _moe_experts(_gather_rows(hn_all, row_tok), sched, w_mlp1[0], b_mlp1[0], w_mlp2[0], b_mlp2[0])
    y_p = _combine(h_p, ys, pos[:t_p], tg_p, g_fin, tm=128)
    y_s = _combine(h_s, ys, pos[t_p:], tg_s, g_fin, tm=t_s)

    n_hist = CONV_W - 1
    conv_p = ut_p.reshape(batch, seq // 256, SUBLANES, D_CONV)[:, -1, SUBLANES - n_hist:]
    conv_s = u_s.reshape(dec_b, dec_t, D_CONV)[:, dec_t - n_hist:]
    return (y_p.reshape(batch, seq, d), y_s.reshape(dec_b, dec_t, d),
            k.reshape(1, batch, seq, N_HEADS, HEAD_DIM), v.reshape(1, batch, seq, N_HEADS, HEAD_DIM),
            conv_p[None], k_s.reshape(1, dec_b, dec_t, N_HEADS, HEAD_DIM),
            v_s.reshape(1, dec_b, dec_t, N_HEADS, HEAD_DIM), conv_s[None])
```
